```python
import math
import jax, jax.numpy as jnp
from jax import lax
import numpy as np

D_MODEL = 1024
BATCH = 32
SEQ = 2048
DEPTH = 1
DEC_BATCH = 8
DEC_SEQ = 32
PAST_LEN = 4096

CHUNK = 64
N_META = 16
CONV_CH = 512
CONV_WIDTH = 31
SB_HEADS = 8
SB_HEAD_DIM = 64
SB_WIDTH = SB_HEADS * SB_HEAD_DIM
MIX_WIDTH = CONV_CH + SB_WIDTH
IN_WIDTH = 2 * CONV_CH + 3 * SB_WIDTH
Q_BLOCK = 128
N_GROUPS = 4
EXPERTS_PER_GROUP = 8
N_EXPERTS = N_GROUPS * EXPERTS_PER_GROUP
TOP_K = 2
D_EXPERT = 256
EPS = 1e-6

kernel_name = 'hymba_conformer_stickbreak_hmoe_stream'


def rmsnorm(x, g):
    xf = x.astype(jnp.float32)
    y = xf * lax.rsqrt(jnp.mean(xf * xf, axis=-1, keepdims=True) + EPS)
    return (y * g.astype(jnp.float32)).astype(x.dtype)


def layernorm(x, g, b):
    xf = x.astype(jnp.float32)
    mu = jnp.mean(xf, axis=-1, keepdims=True)
    var = jnp.mean(jnp.square(xf - mu), axis=-1, keepdims=True)
    y = (xf - mu) * lax.rsqrt(var + EPS)
    return (y * g.astype(jnp.float32) + b.astype(jnp.float32)).astype(x.dtype)


def causal_dwconv(u_hist, w, b):
    y = lax.conv_general_dilated(u_hist, w[:, None, :], window_strides=(1,), padding='VALID',
                                 dimension_numbers=('NWC', 'WIO', 'NWC'),
                                 feature_group_count=CONV_CH)
    return y + b


def sb_block(q, k, v, q_start):
    scale = 1.0 / math.sqrt(SB_HEAD_DIM)
    z = jnp.einsum('bqhd,bkhd->bhqk', q.astype(jnp.float32), k.astype(jnp.float32)) * scale
    t = q_start + jnp.arange(q.shape[1])
    s = jnp.arange(k.shape[1])
    vis = s[None, :] < t[:, None]
    log_keep = jnp.where(vis, jax.nn.log_sigmoid(-z), 0.0)
    later = lax.cumsum(log_keep, axis=3, reverse=True) - log_keep
    w = jnp.where(vis, jnp.exp(jax.nn.log_sigmoid(z) + later), 0.0)
    return jnp.einsum('bhqk,bkhd->bqhd', w, v.astype(jnp.float32))


def stick_breaking(q, k, v, q_start):
    tq = q.shape[1]
    outs = []
    for s0 in range(0, tq, Q_BLOCK):
        s1 = min(s0 + Q_BLOCK, tq)
        n_keys = q_start + s1
        outs.append(sb_block(q[:, s0:s1], k[:, :n_keys], v[:, :n_keys], q_start + s0))
    return jnp.concatenate(outs, axis=1)


def head_rmsnorm(o, g):
    y = o * lax.rsqrt(jnp.mean(o * o, axis=-1, keepdims=True) + EPS)
    b, t = o.shape[0], o.shape[1]
    return y.reshape(b, t, SB_WIDTH) * g.astype(jnp.float32)


def hier_moe(h, w_rg, b_rg, w_re, b_re, w_e_in, w_e_out):
    b, t, d = h.shape
    hf = h.reshape(b * t, d)
    lg = (hf @ w_rg).astype(jnp.float32) + b_rg.astype(jnp.float32)
    p_g = jax.nn.softmax(lg, axis=-1)
    g_sel = jnp.argmax(lg, axis=-1)
    p_g_sel = jnp.max(p_g, axis=-1, keepdims=True)
    le = ((hf @ w_re).astype(jnp.float32) + b_re.astype(jnp.float32)).reshape(-1, N_GROUPS, EXPERTS_PER_GROUP)
    le_sel = jnp.einsum('nge,ng->ne', le, jax.nn.one_hot(g_sel, N_GROUPS, dtype=jnp.float32))
    p_e = jax.nn.softmax(le_sel, axis=-1)
    top_p, top_i = lax.top_k(p_e, TOP_K)
    top_p = top_p / jnp.sum(top_p, axis=-1, keepdims=True)
    expert_id = g_sel[:, None] * EXPERTS_PER_GROUP + top_i
    gates = jnp.sum(jax.nn.one_hot(expert_id, N_EXPERTS, dtype=jnp.float32)
                    * (p_g_sel * top_p)[..., None], axis=1)
    out = jnp.zeros((b * t, d), jnp.float32)
    for e in range(N_EXPERTS):
        gu = hf @ w_e_in[e]
        act = jax.nn.silu(gu[:, :D_EXPERT]) * gu[:, D_EXPERT:]
        out = out + gates[:, e:e + 1] * (act @ w_e_out[e]).astype(jnp.float32)
    return out.astype(h.dtype).reshape(b, t, d)


def layer(x, conv_hist, k_past, v_past, p):
    (g_mix, w_in, conv_w, conv_b, ln_g, ln_b, g_sb, w_out,
     g_ffn, w_rg, b_rg, w_re, b_re, w_e_in, w_e_out) = p
    b, t, _ = x.shape
    h = rmsnorm(x, g_mix)
    proj = h @ w_in
    a_val, a_gate, q, k, v = jnp.split(
        proj, [CONV_CH, 2 * CONV_CH, 2 * CONV_CH + SB_WIDTH, 2 * CONV_CH + 2 * SB_WIDTH], axis=-1)
    u = a_val * jax.nn.sigmoid(a_gate)
    u_hist = jnp.concatenate([conv_hist, u], axis=1)
    a_out = jax.nn.silu(layernorm(causal_dwconv(u_hist, conv_w, conv_b), ln_g, ln_b))
    q = q.reshape(b, t, SB_HEADS, SB_HEAD_DIM)
    k = k.reshape(b, t, SB_HEADS, SB_HEAD_DIM)
    v = v.reshape(b, t, SB_HEADS, SB_HEAD_DIM)
    if k_past is None:
        k_all, v_all, q_start = k, v, 0
    else:
        k_all = jnp.concatenate([k_past, k], axis=1)
        v_all = jnp.concatenate([v_past, v], axis=1)
        q_start = k_past.shape[1]
    b_out = head_rmsnorm(stick_breaking(q, k_all, v_all, q_start), g_sb).astype(x.dtype)
    x = x + jnp.concatenate([a_out, b_out], axis=-1) @ w_out
    x = x + hier_moe(rmsnorm(x, g_ffn), w_rg, b_rg, w_re, b_re, w_e_in, w_e_out)
    return x, k, v, u_hist[:, -(CONV_WIDTH - 1):]


def setup_inputs(seed: int = 0) -> dict:
    key = jax.random.key(seed)
    ks = jax.random.split(key, 24)
    f32 = jnp.float32
    nrm = lambda k, shape, s: jax.random.normal(k, shape, f32) * s
    return {
        'x_prompt': nrm(ks[0], (BATCH, SEQ, D_MODEL), 1.0),
        'x_sample': nrm(ks[1], (DEC_BATCH, DEC_SEQ, D_MODEL), 1.0),
        'cache_k': nrm(ks[2], (DEPTH, DEC_BATCH, PAST_LEN, SB_HEADS, SB_HEAD_DIM), 1.0),
        'cache_v': nrm(ks[3], (DEPTH, DEC_BATCH, PAST_LEN, SB_HEADS, SB_HEAD_DIM), 1.0),
        'cache_conv': nrm(ks[4], (DEPTH, DEC_BATCH, CONV_WIDTH - 1, CONV_CH), 0.5),
        'meta_tokens': nrm(ks[5], (N_META, D_MODEL), 1.0),
        'g_mix': 1.0 + nrm(ks[6], (DEPTH, D_MODEL), 0.02),
        'w_in': nrm(ks[7], (DEPTH, D_MODEL, IN_WIDTH), D_MODEL ** -0.5),
        'conv_w': nrm(ks[8], (DEPTH, CONV_WIDTH, CONV_CH), CONV_WIDTH ** -0.5),
        'conv_b': nrm(ks[9], (DEPTH, CONV_CH), 0.02),
        'conv_ln_g': 1.0 + nrm(ks[10], (DEPTH, CONV_CH), 0.02),
        'conv_ln_b': nrm(ks[11], (DEPTH, CONV_CH), 0.02),
        'g_sb': 1.0 + nrm(ks[12], (DEPTH, SB_WIDTH), 0.02),
        'w_out': nrm(ks[13], (DEPTH, MIX_WIDTH, D_MODEL), MIX_WIDTH ** -0.5),
        'g_ffn': 1.0 + nrm(ks[14], (DEPTH, D_MODEL), 0.02),
        'w_router_group': nrm(ks[15], (DEPTH, D_MODEL, N_GROUPS), D_MODEL ** -0.5),
        'b_router_group': nrm(ks[16], (DEPTH, N_GROUPS), 0.01),
        'w_router_expert': nrm(ks[17], (DEPTH, D_MODEL, N_EXPERTS), D_MODEL ** -0.5),
        'b_router_expert': nrm(ks[18], (DEPTH, N_EXPERTS), 0.01),
        'w_expert_in': nrm(ks[19], (DEPTH, N_EXPERTS, D_MODEL, 2 * D_EXPERT), D_MODEL ** -0.5),
        'w_expert_out': nrm(ks[20], (DEPTH, N_EXPERTS, D_EXPERT, D_MODEL), D_EXPERT ** -0.5),
        'g_final': 1.0 + nrm(ks[21], (D_MODEL,), 0.02),
    }


def reference(x_prompt, x_sample, cache_k, cache_v, cache_conv, meta_tokens,
              g_mix, w_in, conv_w, conv_b, conv_ln_g, conv_ln_b, g_sb, w_out,
              g_ffn, w_router_group, b_router_group, w_router_expert, b_router_expert,
              w_expert_in, w_expert_out, g_final):
    meta = jnp.broadcast_to(meta_tokens[None].astype(x_prompt.dtype), (x_prompt.shape[0], N_META, D_MODEL))
    xp = jnp.concatenate([meta, x_prompt], axis=1)
    xs = x_sample
    kp_l, vp_l, cp_l, ks_l, vs_l, cs_l = [], [], [], [], [], []
    for l in range(DEPTH):
        p = (g_mix[l], w_in[l], conv_w[l], conv_b[l], conv_ln_g[l], conv_ln_b[l], g_sb[l], w_out[l],
             g_ffn[l], w_router_group[l], b_router_group[l], w_router_expert[l], b_router_expert[l],
             w_expert_in[l], w_expert_out[l])
        zero_hist = jnp.zeros((xp.shape[0], CONV_WIDTH - 1, CONV_CH), xp.dtype)
        xp, kp, vp, cp = layer(xp, zero_hist, None, None, p)
        xs, ksn, vsn, csn = layer(xs, cache_conv[l], cache_k[l], cache_v[l], p)
        kp_l.append(kp); vp_l.append(vp); cp_l.append(cp)
        ks_l.append(ksn); vs_l.append(vsn); cs_l.append(csn)
    y_prompt = rmsnorm(xp, g_final)[:, N_META:]
    y_sample = rmsnorm(xs, g_final)
    new_k_prompt = jnp.stack(kp_l)
    new_v_prompt = jnp.stack(vp_l)
    new_conv_prompt = jnp.stack(cp_l)
    new_k_sample = jnp.stack(ks_l)
    new_v_sample = jnp.stack(vs_l)
    new_conv_sample = jnp.stack(cs_l)
    return (y_prompt, y_sample, new_k_prompt, new_v_prompt, new_conv_prompt,
            new_k_sample, new_v_sample, new_conv_sample)
```

```python
import functools

import jax
import jax.numpy as jnp
from jax import lax
from jax.experimental import pallas as pl
from jax.experimental.pallas import tpu as pltpu

D_MODEL = 1024
N_META = 16
CONV_CH = 512
CONV_WIDTH = 31
CONV_HIST = CONV_WIDTH - 1
SB_HEADS = 8
SB_HEAD_DIM = 64
SB_WIDTH = SB_HEADS * SB_HEAD_DIM
N_GROUPS = 4
EXPERTS_PER_GROUP = 8
N_EXPERTS = N_GROUPS * EXPERTS_PER_GROUP
D_EXPERT = 256
EPS = 1e-6

LANES = 128
HIST_PAD = 32
KEY_BLOCK = 128
VMEM_LIMIT = 52 * 1024 * 1024

_F32 = jnp.float32
_BF16 = jnp.bfloat16


def _dot(a, b):
    return jnp.dot(a, b, preferred_element_type=_F32)


def _rms(x, g):
    return x * lax.rsqrt(jnp.mean(x * x, axis=-1, keepdims=True) + EPS) * g


def _proj_store(h, w_ref, u_ref, q_ref, k_ref, v_ref):
    c, s = CONV_CH, SB_WIDTH
    val = _dot(h, w_ref[:, 0:c])
    gate = _dot(h, w_ref[:, c:2 * c])
    u_ref[0] = val * jax.nn.sigmoid(gate)
    q_ref[0] = (_dot(h, w_ref[:, 2 * c:2 * c + s]) * (SB_HEAD_DIM ** -0.5)).astype(_BF16)
    k_ref[0] = _dot(h, w_ref[:, 2 * c + s:2 * c + 2 * s])
    v_ref[0] = _dot(h, w_ref[:, 2 * c + 2 * s:2 * c + 3 * s])


def _proj_prompt_kernel(x_ref, meta_ref, g_ref, w_ref, u_ref, q_ref, k_ref, v_ref, h_ref, *, tile):
    t = pl.program_id(1)
    g = g_ref[...]
    main0 = pl.multiple_of(t * tile, 8)
    h_ref[N_META:, :] = _rms(x_ref[0, pl.ds(main0, tile - N_META), :], g).astype(_BF16)
    head0 = pl.multiple_of(jnp.maximum(t * tile - N_META, 0), 8)
    xh = jnp.where(t == 0, meta_ref[...], x_ref[0, pl.ds(head0, N_META), :])
    h_ref[0:N_META, :] = _rms(xh, g).astype(_BF16)
    _proj_store(h_ref[...], w_ref, u_ref, q_ref, k_ref, v_ref)


def _proj_sample_kernel(x_ref, g_ref, w_ref, u_ref, q_ref, k_ref, v_ref):
    _proj_store(_rms(x_ref[0], g_ref[...]).astype(_BF16), w_ref, u_ref, q_ref, k_ref, v_ref)


def _proj_prompt(x, meta, g_mix, w_in):
    b, t_x, d = x.shape
    t_all = t_x + N_META
    tile = t_all // 3
    assert tile * 3 == t_all and tile % 16 == 0
    in_w = w_in.shape[1]
    outs = [jax.ShapeDtypeStruct((b, t_all, CONV_CH), _F32),
            jax.ShapeDtypeStruct((b, t_all, SB_WIDTH), _BF16),
            jax.ShapeDtypeStruct((b, t_all, SB_WIDTH), _F32),
            jax.ShapeDtypeStruct((b, t_all, SB_WIDTH), _F32)]
    out_spec = pl.BlockSpec((1, tile, CONV_CH), lambda i, j: (i, j, 0))
    return pl.pallas_call(
        functools.partial(_proj_prompt_kernel, tile=tile),
        grid=(b, 3),
        in_specs=[pl.BlockSpec((1, t_x, d), lambda i, j: (i, 0, 0)),
                  pl.BlockSpec((N_META, d), lambda i, j: (0, 0)),
                  pl.BlockSpec((1, d), lambda i, j: (0, 0)),
                  pl.BlockSpec((d, in_w), lambda i, j: (0, 0))],
        out_specs=[out_spec] * 4,
        out_shape=outs,
        scratch_shapes=[pltpu.VMEM((tile, d), _BF16)],
        compiler_params=pltpu.CompilerParams(
            dimension_semantics=("arbitrary", "arbitrary"), vmem_limit_bytes=VMEM_LIMIT),
        name="proj_prompt",
    )(x, meta, g_mix, w_in)


def _proj_sample(x, g_mix, w_in):
    b, t, d = x.shape
    in_w = w_in.shape[1]
    outs = [jax.ShapeDtypeStruct((b, t, CONV_CH), _F32),
            jax.ShapeDtypeStruct((b, t, SB_WIDTH), _BF16),
            jax.ShapeDtypeStruct((b, t, SB_WIDTH), _F32),
            jax.ShapeDtypeStruct((b, t, SB_WIDTH), _F32)]
    out_spec = pl.BlockSpec((1, t, CONV_CH), lambda i: (i, 0, 0))
    return pl.pallas_call(
        _proj_sample_kernel,
        grid=(b,),
        in_specs=[pl.BlockSpec((1, t, d), lambda i: (i, 0, 0)),
                  pl.BlockSpec((1, d), lambda i: (0, 0)),
                  pl.BlockSpec((d, in_w), lambda i: (0, 0))],
        out_specs=[out_spec] * 4,
        out_shape=outs,
        compiler_params=pltpu.CompilerParams(
            dimension_semantics=("arbitrary",), vmem_limit_bytes=VMEM_LIMIT),
        name="proj_sample",
    )(x, g_mix, w_in)


def _conv_kernel(u_ref, hist_ref, cw_ref, cb_ref, lg_ref, lb_ref, a_ref, nc_ref, uh_ref, *, t_len, rows):
    uh_ref[0:HIST_PAD, :] = hist_ref[0]
    uh_ref[HIST_PAD:HIST_PAD + t_len, :] = u_ref[0]
    nc_ref[0] = uh_ref[t_len:t_len + HIST_PAD, :]
    cw = cw_ref[...]
    cb, lg, lb = cb_ref[...], lg_ref[...], lb_ref[...]
    lead = HIST_PAD - CONV_HIST

    def chunk(c, carry):
        r0 = pl.multiple_of(c * rows, 16)
        win = uh_ref[pl.ds(r0, rows + HIST_PAD), :]
        acc = jnp.broadcast_to(cb, (rows, CONV_CH))
        for s in range(8):
            taps = [k for k in range(CONV_WIDTH) if (lead + k) % 8 == s]
            shifted = win if s == 0 else win[s:s + rows + HIST_PAD - 8, :]
            for k in taps:
                off = lead + k - s
                acc = acc + cw[k:k + 1, :] * shifted[off:off + rows, :]
        mu = jnp.mean(acc, axis=-1, keepdims=True)
        dlt = acc - mu
        var = jnp.mean(dlt * dlt, axis=-1, keepdims=True)
        y = dlt * lax.rsqrt(var + EPS) * lg + lb
        a_ref[0, pl.ds(r0, rows), :] = (y * jax.nn.sigmoid(y)).astype(_BF16)
        return carry

    lax.fori_loop(0, t_len // rows, chunk, 0)


def _conv(u, hist_padded, conv_w, conv_b, ln_g, ln_b, rows):
    b, t_len, c = u.shape
    assert t_len % rows == 0 and rows % 16 == 0
    vec = pl.BlockSpec((1, c), lambda i: (0, 0))
    return pl.pallas_call(
        functools.partial(_conv_kernel, t_len=t_len, rows=rows),
        grid=(b,),
        in_specs=[pl.BlockSpec((1, t_len, c), lambda i: (i, 0, 0)),
                  pl.BlockSpec((1, HIST_PAD, c), lambda i: (i, 0, 0)),
                  pl.BlockSpec((CONV_WIDTH, c), lambda i: (0, 0)),
                  vec, vec, vec],
        out_specs=[pl.BlockSpec((1, t_len, c), lambda i: (i, 0, 0)),
                   pl.BlockSpec((1, HIST_PAD, c), lambda i: (i, 0, 0))],
        out_shape=[jax.ShapeDtypeStruct((b, t_len, c), _BF16),
                   jax.ShapeDtypeStruct((b, HIST_PAD, c), _F32)],
        scratch_shapes=[pltpu.VMEM((HIST_PAD + t_len, c), _F32)],
        compiler_params=pltpu.CompilerParams(
            dimension_semantics=("arbitrary",), vmem_limit_bytes=VMEM_LIMIT),
        name=f"conv_t{t_len}",
    )(u, hist_padded, conv_w, conv_b, ln_g, ln_b)


def _sb_step(m, state, qh, kb_ref, vh_refs, tri, vis):
    carries, acc = state
    r0 = pl.multiple_of(m * KEY_BLOCK, KEY_BLOCK)
    kblk = kb_ref[pl.ds(r0, KEY_BLOCK), :]
    new_carries = []
    for h in range(2):
        z = lax.dot_general(qh[h], kblk, (((1,), (1,)), ((), ())), preferred_element_type=_F32)
        log_keep = jnp.minimum(-z, 0.0) - jnp.log(1.0 + jnp.exp(-jnp.abs(z)))
        if vis is not None:
            log_keep = jnp.where(vis, log_keep, 0.0)
        hi = log_keep.astype(_BF16)
        lo = (log_keep - hi.astype(_F32)).astype(_BF16)
        cum_tot = _dot(jnp.concatenate([hi, lo], axis=1), tri)
        cum = cum_tot[:, :KEY_BLOCK] + carries[h]
        w = jnp.exp(z + cum)
        if vis is not None:
            w = jnp.where(vis, w, 0.0)
        acc = acc + _dot(w.astype(_BF16), vh_refs[h][pl.ds(r0, KEY_BLOCK), :])
        new_carries.append(carries[h] + cum_tot[:, KEY_BLOCK:])
    return (tuple(new_carries), acc)


def _sb_kernel(*refs, mq, nq, q_off, front_pad, n_past, has_first):
    if n_past:
        q_ref, kn_ref, vn_ref, kp_ref, vp_ref, g_ref, tri_ref, o_ref, kb_ref, v0_ref, v1_ref = refs
    else:
        q_ref, kn_ref, vn_ref, g_ref, tri_ref, o_ref, kb_ref, v0_ref, v1_ref = refs
    t_new = kn_ref.shape[1]
    rows = kb_ref.shape[0]
    lane_row = lax.broadcasted_iota(jnp.int32, (1, LANES), 1) < SB_HEAD_DIM

    def fill(r0, n, k, v):
        kb_ref[r0:r0 + n, :] = k.astype(_BF16)
        v0_ref[r0:r0 + n, :] = jnp.where(lane_row, v, 0.0).astype(_BF16)
        v1_ref[r0:r0 + n, :] = jnp.where(lane_row, 0.0, v).astype(_BF16)

    new0 = front_pad + n_past
    if front_pad:
        fill(0, front_pad, jnp.zeros((front_pad, LANES), _F32), jnp.zeros((front_pad, LANES), _F32))
    if n_past:
        fill(front_pad, n_past, kp_ref[0], vp_ref[0])
    fill(new0, t_new, kn_ref[0], vn_ref[0])
    tail = rows - new0 - t_new
    if tail:
        fill(new0 + t_new, tail, jnp.zeros((tail, LANES), _F32), jnp.zeros((tail, LANES), _F32))
    if q_off:
        o_ref[0, 0:q_off, :] = jnp.zeros((q_off, LANES), _BF16)

    tri = tri_ref[...]
    g = g_ref[...]
    lane = lax.broadcasted_iota(jnp.int32, (mq, LANES), 1)
    row = lax.broadcasted_iota(jnp.int32, (mq, LANES), 0)
    lane_lo = lane < SB_HEAD_DIM
    causal = lane < row
    first_vis = lane >= front_pad
    diag0 = (new0 + q_off) // KEY_BLOCK
    assert (new0 + q_off) % KEY_BLOCK == 0 and mq <= KEY_BLOCK
    m_lo = 1 if has_first else 0

    def qblock(i, carry):
        q0 = pl.multiple_of(q_off + i * mq, 16)
        qb = q_ref[0, pl.ds(q0, mq), :]
        zero_q = jnp.zeros_like(qb)
        qh = (jnp.where(lane_lo, qb, zero_q), jnp.where(lane_lo, zero_q, qb))
        vh = (v0_ref, v1_ref)
        zeros = jnp.zeros((mq, LANES), _F32)
        md = diag0 + i * (mq // KEY_BLOCK if mq >= KEY_BLOCK else 0)
        state = ((zeros, zeros), zeros)
        state = _sb_step(md, state, qh, kb_ref, vh, tri, causal)
        state = lax.fori_loop(
            0, md - m_lo, lambda n, st: _sb_step(md - 1 - n, st, qh, kb_ref, vh, tri, None), state)
        if has_first:
            state = _sb_step(0, state, qh, kb_ref, vh, tri, first_vis)
        o = state[1]
        o2 = o * o
        s_all = jnp.sum(o2, axis=-1, keepdims=True)
        s_lo = jnp.sum(jnp.where(lane_lo, o2, 0.0), axis=-1, keepdims=True)
        inv = jnp.where(lane_lo, lax.rsqrt(s_lo / SB_HEAD_DIM + EPS),
                        lax.rsqrt((s_all - s_lo) / SB_HEAD_DIM + EPS))
        o_ref[0, pl.ds(q0, mq), :] = (o * inv * g).astype(_BF16)
        return carry

    lax.fori_loop(0, nq, qblock, 0)


def _tri_matrix():
    r = lax.broadcasted_iota(jnp.int32, (2 * KEY_BLOCK, 2 * KEY_BLOCK), 0) % KEY_BLOCK
    c = lax.broadcasted_iota(jnp.int32, (2 * KEY_BLOCK, 2 * KEY_BLOCK), 1)
    return jnp.where((c >= KEY_BLOCK) | (r >= c), 1.0, 0.0).astype(_BF16)


def _stick_breaking(q, k_new, v_new, k_past, v_past, g_sb, *, mq, q_off, front_pad, has_first):
    b, t_new, _ = k_new.shape
    n_past = 0 if k_past is None else k_past.shape[1]
    nq = (t_new - q_off) // mq
    assert nq * mq == t_new - q_off
    rows = pl.cdiv(front_pad + n_past + t_new, KEY_BLOCK) * KEY_BLOCK
    pairs = SB_WIDTH // LANES
    new_spec = pl.BlockSpec((1, t_new, LANES), lambda i, p: (i, 0, p))
    in_specs = [new_spec, new_spec, new_spec]
    args = [q, k_new, v_new]
    if n_past:
        past_spec = pl.BlockSpec((1, n_past, LANES), lambda i, p: (i, 0, p))
        in_specs += [past_spec, past_spec]
        args += [k_past, v_past]
    in_specs += [pl.BlockSpec((1, LANES), lambda i, p: (0, p)),
                 pl.BlockSpec((2 * KEY_BLOCK, 2 * KEY_BLOCK), lambda i, p: (0, 0))]
    args += [g_sb, _tri_matrix()]
    return pl.pallas_call(
        functools.partial(_sb_kernel, mq=mq, nq=nq, q_off=q_off, front_pad=front_pad,
                          n_past=n_past, has_first=has_first),
        grid=(b, pairs),
        in_specs=in_specs,
        out_specs=pl.BlockSpec((1, t_new, LANES), lambda i, p: (i, 0, p)),
        out_shape=jax.ShapeDtypeStruct((b, t_new, SB_WIDTH), _BF16),
        scratch_shapes=[pltpu.VMEM((rows, LANES), _BF16)] * 3,
        compiler_params=pltpu.CompilerParams(
            dimension_semantics=("arbitrary", "arbitrary"), vmem_limit_bytes=VMEM_LIMIT),
        name=f"stick_breaking_t{t_new}",
    )(*args)


def _mix_kernel(a_ref, b_ref, x_ref, wa_ref, wb_ref, g_ref, wr_ref, br_ref,
                x2_ref, h2_ref, gates_ref, *, n, off):
    r = pl.program_id(1)
    a0 = pl.multiple_of(off + r * n, 16)
    x2 = (x_ref[0] + _dot(a_ref[0, pl.ds(a0, n), :], wa_ref[...])
          + _dot(b_ref[0, pl.ds(a0, n), :], wb_ref[...]))
    x2_ref[0] = x2
    hf = _rms(x2, g_ref[...])
    h2_ref[0] = hf.astype(_BF16)
    logits = jnp.dot(hf, wr_ref[...], preferred_element_type=_F32,
                     precision=lax.Precision.HIGHEST) + br_ref[...]
    lane = lax.broadcasted_iota(jnp.int32, (n, LANES), 1)
    neg = -jnp.inf
    big = 4 * LANES
    is_g = (lane >= N_EXPERTS) & (lane < N_EXPERTS + N_GROUPS)
    lg = jnp.where(is_g, logits, neg)
    g_max = jnp.max(lg, axis=-1, keepdims=True)
    p_g_sel = 1.0 / jnp.sum(jnp.where(is_g, jnp.exp(lg - g_max), 0.0), axis=-1, keepdims=True)
    g_sel = jnp.min(jnp.where(lg == g_max, lane, big), axis=-1, keepdims=True) - N_EXPERTS
    in_grp = (lane < N_EXPERTS) & ((lane // EXPERTS_PER_GROUP) == g_sel)
    le = jnp.where(in_grp, logits, neg)
    e_max = jnp.max(le, axis=-1, keepdims=True)
    pe = jnp.where(in_grp, jnp.exp(le - e_max), 0.0)
    pe = pe / jnp.sum(pe, axis=-1, keepdims=True)
    pe = jnp.where(in_grp, pe, -1.0)
    top1 = jnp.max(pe, axis=-1, keepdims=True)
    i1 = jnp.min(jnp.where(pe == top1, lane, big), axis=-1, keepdims=True)
    pe_rest = jnp.where(lane == i1, -1.0, pe)
    top2 = jnp.max(pe_rest, axis=-1, keepdims=True)
    i2 = jnp.min(jnp.where(pe_rest == top2, lane, big), axis=-1, keepdims=True)
    denom = top1 + top2
    gates_ref[0] = jnp.where(lane == i1, top1 / denom, jnp.where(lane == i2, top2 / denom, 0.0)) * p_g_sel


def _mix(a, b_out, x, wa, wb, g_ffn, wr, br, *, n, off):
    bsz, t_x, d = x.shape
    t_a = a.shape[1]
    assert t_x % n == 0
    res = pl.BlockSpec((1, t_a, CONV_CH), lambda i, r: (i, 0, 0))
    tile = pl.BlockSpec((1, n, d), lambda i, r: (i, r, 0))
    full = lambda shape: pl.BlockSpec(shape, lambda i, r: (0,) * len(shape))
    return pl.pallas_call(
        functools.partial(_mix_kernel, n=n, off=off),
        grid=(bsz, t_x // n),
        in_specs=[res, res, tile, full(wa.shape), full(wb.shape), full((1, d)),
                  full(wr.shape), full((1, LANES))],
        out_specs=[tile, tile, pl.BlockSpec((1, n, LANES), lambda i, r: (i, r, 0))],
        out_shape=[jax.ShapeDtypeStruct((bsz, t_x, d), _F32),
                   jax.ShapeDtypeStruct((bsz, t_x, d), _BF16),
                   jax.ShapeDtypeStruct((bsz, t_x, LANES), _F32)],
        compiler_params=pltpu.CompilerParams(
            dimension_semantics=("arbitrary", "arbitrary"), vmem_limit_bytes=VMEM_LIMIT),
        name=f"mix_t{t_x}",
    )(a, b_out, x, wa, wb, g_ffn, wr, br)


def _moe_kernel(h_ref, gates_ref, x2_ref, win_ref, wout_ref, gf_ref, y_ref, acc_ref):
    e = pl.program_id(1)

    @pl.when(e == 0)
    def _():
        acc_ref[...] = jnp.zeros_like(acc_ref)

    gu = _dot(h_ref[...], win_ref[0])
    act = (jax.nn.silu(gu[:, :D_EXPERT]) * gu[:, D_EXPERT:]).astype(_BF16)
    gates = gates_ref[...]
    lane = lax.broadcasted_iota(jnp.int32, gates.shape, 1)
    gate_e = jnp.sum(jnp.where(lane == e, gates, 0.0), axis=-1, keepdims=True)
    acc_ref[...] += gate_e * _dot(act, wout_ref[0])

    @pl.when(e == pl.num_programs(1) - 1)
    def _():
        y_ref[...] = _rms(x2_ref[...] + acc_ref[...], gf_ref[...])


def _moe(h2, gates, x2, w_e_in, w_e_out, g_final, *, tm):
    rows, d = h2.shape
    assert rows % tm == 0
    n_e = w_e_in.shape[0]
    tok = lambda w: pl.BlockSpec((tm, w), lambda i, e: (i, 0))
    return pl.pallas_call(
        _moe_kernel,
        grid=(rows // tm, n_e),
        in_specs=[tok(d), tok(LANES), tok(d),
                  pl.BlockSpec((1, d, 2 * D_EXPERT), lambda i, e: (e, 0, 0)),
                  pl.BlockSpec((1, D_EXPERT, d), lambda i, e: (e, 0, 0)),
                  pl.BlockSpec((1, d), lambda i, e: (0, 0))],
        out_specs=tok(d),
        out_shape=jax.ShapeDtypeStruct((rows, d), _F32),
        scratch_shapes=[pltpu.VMEM((tm, d), _F32)],
        compiler_params=pltpu.CompilerParams(
            dimension_semantics=("arbitrary", "arbitrary"), vmem_limit_bytes=VMEM_LIMIT),
        name=f"moe_r{rows}",
    )(h2, gates, x2, w_e_in, w_e_out, g_final)


def kernel(x_prompt, x_sample, cache_k, cache_v, cache_conv, meta_tokens, g_mix, w_in, conv_w, conv_b,
           conv_ln_g, conv_ln_b, g_sb, w_out, g_ffn, w_router_group, b_router_group, w_router_expert,
           b_router_expert, w_expert_in, w_expert_out, g_final):
    depth = w_in.shape[0]
    assert depth == 1, "single-layer trunk"
    bp, tp, d = x_prompt.shape
    bs, ts, _ = x_sample.shape
    n_past = cache_k.shape[2]

    w_in_b = w_in[0].astype(_BF16)
    wa = w_out[0, :CONV_CH].astype(_BF16)
    wb = w_out[0, CONV_CH:].astype(_BF16)
    w_e_in = w_expert_in[0].astype(_BF16)
    w_e_out = w_expert_out[0].astype(_BF16)
    pad = LANES - N_EXPERTS - N_GROUPS
    wr = jnp.concatenate([w_router_expert[0], w_router_group[0], jnp.zeros((d, pad), _F32)], axis=1)
    br = jnp.concatenate([b_router_expert[0], b_router_group[0], jnp.zeros((pad,), _F32)])[None]
    g_mix2, g_ffn2, g_fin2, g_sb2 = g_mix[0][None], g_ffn[0][None], g_final[None], g_sb[0][None]
    cb, lg, lb = conv_b[0][None], conv_ln_g[0][None], conv_ln_b[0][None]

    def tail(a, b_out, x, *, n, off, tm):
        x2, h2, gates = _mix(a, b_out, x, wa, wb, g_ffn2, wr, br, n=n, off=off)
        rows = x.shape[0] * x.shape[1]
        y = _moe(h2.reshape(rows, d), gates.reshape(rows, LANES), x2.reshape(rows, d),
                 w_e_in, w_e_out, g_fin2, tm=tm)
        return y.reshape(x.shape)

    u_p, q_p, k_p, v_p = _proj_prompt(x_prompt, meta_tokens, g_mix2, w_in_b)
    a_p, nc_p = _conv(u_p, jnp.zeros((bp, HIST_PAD, CONV_CH), _F32), conv_w[0], cb, lg, lb, rows=48)
    b_p = _stick_breaking(q_p, k_p, v_p, None, None, g_sb2, mq=KEY_BLOCK, q_off=N_META,
                          front_pad=KEY_BLOCK - N_META, has_first=True)
    y_p = tail(a_p, b_p, x_prompt, n=512, off=N_META, tm=1024)

    u_s, q_s, k_s, v_s = _proj_sample(x_sample, g_mix2, w_in_b)
    hist = jnp.pad(cache_conv[0], ((0, 0), (HIST_PAD - CONV_HIST, 0), (0, 0)))
    a_s, nc_s = _conv(u_s, hist, conv_w[0], cb, lg, lb, rows=ts)
    b_s = _stick_breaking(q_s, k_s, v_s, cache_k[0].reshape(bs, n_past, SB_WIDTH),
                          cache_v[0].reshape(bs, n_past, SB_WIDTH), g_sb2, mq=ts, q_off=0,
                          front_pad=0, has_first=False)
    y_s = tail(a_s, b_s, x_sample, n=ts, off=0, tm=bs * ts)

    heads = (SB_HEADS, SB_HEAD_DIM)
    lead = HIST_PAD - CONV_HIST
    return (y_p, y_s,
            k_p.reshape(1, bp, tp + N_META, *heads), v_p.reshape(1, bp, tp + N_META, *heads),
            nc_p[None, :, lead:],
            k_s.reshape(1, bs, ts, *heads), v_s.reshape(1, bs, ts, *heads),
            nc_s[None, :, lead:])
```

```python
import functools

import jax
import jax.numpy as jnp
from jax import lax
from jax.experimental import pallas as pl
from jax.experimental.pallas import tpu as pltpu

D_MODEL = 1024
N_META = 16
CONV_CH = 512
CONV_WIDTH = 31
CONV_HIST = CONV_WIDTH - 1
SB_HEADS = 8
SB_HEAD_DIM = 64
SB_WIDTH = SB_HEADS * SB_HEAD_DIM
N_GROUPS = 4
EXPERTS_PER_GROUP = 8
N_EXPERTS = N_GROUPS * EXPERTS_PER_GROUP
D_EXPERT = 256
EPS = 1e-6

LANES = 128
HIST_PAD = 32
KEY_BLOCK = 128
KEY_CHUNK = 256
MASKED_Z = -1e30
EXIT_LOG = 105.0
VMEM_LIMIT = 52 * 1024 * 1024

_F32 = jnp.float32
_BF16 = jnp.bfloat16


def _dot(a, b):
    return jnp.dot(a, b, preferred_element_type=_F32)


def _rms(x, g):
    return x * lax.rsqrt(jnp.mean(x * x, axis=-1, keepdims=True) + EPS) * g


def _proj_store(h, w_ref, u_ref, q_ref, k_ref, v_ref):
    c, s = CONV_CH, SB_WIDTH
    val = _dot(h, w_ref[:, 0:c])
    gate = _dot(h, w_ref[:, c:2 * c])
    u_ref[0] = val * jax.nn.sigmoid(gate)
    q_ref[0] = (_dot(h, w_ref[:, 2 * c:2 * c + s]) * (SB_HEAD_DIM ** -0.5)).astype(_BF16)
    k_ref[0] = _dot(h, w_ref[:, 2 * c + s:2 * c + 2 * s])
    v_ref[0] = _dot(h, w_ref[:, 2 * c + 2 * s:2 * c + 3 * s])


def _proj_prompt_kernel(x_ref, meta_ref, g_ref, w_ref, u_ref, q_ref, k_ref, v_ref, h_ref, *, tile):
    t = pl.program_id(1)
    g = g_ref[...]
    main0 = pl.multiple_of(t * tile, 8)
    h_ref[N_META:, :] = _rms(x_ref[0, pl.ds(main0, tile - N_META), :], g).astype(_BF16)
    head0 = pl.multiple_of(jnp.maximum(t * tile - N_META, 0), 8)
    xh = jnp.where(t == 0, meta_ref[...], x_ref[0, pl.ds(head0, N_META), :])
    h_ref[0:N_META, :] = _rms(xh, g).astype(_BF16)
    _proj_store(h_ref[...], w_ref, u_ref, q_ref, k_ref, v_ref)


def _proj_sample_kernel(x_ref, g_ref, w_ref, u_ref, q_ref, k_ref, v_ref):
    _proj_store(_rms(x_ref[0], g_ref[...]).astype(_BF16), w_ref, u_ref, q_ref, k_ref, v_ref)


def _proj_prompt(x, meta, g_mix, w_in):
    b, t_x, d = x.shape
    t_all = t_x + N_META
    tile = t_all // 3
    assert tile * 3 == t_all and tile % 16 == 0
    in_w = w_in.shape[1]
    outs = [jax.ShapeDtypeStruct((b, t_all, CONV_CH), _F32),
            jax.ShapeDtypeStruct((b, t_all, SB_WIDTH), _BF16),
            jax.ShapeDtypeStruct((b, t_all, SB_WIDTH), _F32),
            jax.ShapeDtypeStruct((b, t_all, SB_WIDTH), _F32)]
    out_spec = pl.BlockSpec((1, tile, CONV_CH), lambda i, j: (i, j, 0))
    return pl.pallas_call(
        functools.partial(_proj_prompt_kernel, tile=tile),
        grid=(b, 3),
        in_specs=[pl.BlockSpec((1, t_x, d), lambda i, j: (i, 0, 0)),
                  pl.BlockSpec((N_META, d), lambda i, j: (0, 0)),
                  pl.BlockSpec((1, d), lambda i, j: (0, 0)),
                  pl.BlockSpec((d, in_w), lambda i, j: (0, 0))],
        out_specs=[out_spec] * 4,
        out_shape=outs,
        scratch_shapes=[pltpu.VMEM((tile, d), _BF16)],
        compiler_params=pltpu.CompilerParams(
            dimension_semantics=("arbitrary", "arbitrary"), vmem_limit_bytes=VMEM_LIMIT),
        name="proj_prompt",
    )(x, meta, g_mix, w_in)


def _proj_sample(x, g_mix, w_in):
    b, t, d = x.shape
    in_w = w_in.shape[1]
    outs = [jax.ShapeDtypeStruct((b, t, CONV_CH), _F32),
            jax.ShapeDtypeStruct((b, t, SB_WIDTH), _BF16),
            jax.ShapeDtypeStruct((b, t, SB_WIDTH), _F32),
            jax.ShapeDtypeStruct((b, t, SB_WIDTH), _F32)]
    out_spec = pl.BlockSpec((1, t, CONV_CH), lambda i: (i, 0, 0))
    return pl.pallas_call(
        _proj_sample_kernel,
        grid=(b,),
        in_specs=[pl.BlockSpec((1, t, d), lambda i: (i, 0, 0)),
                  pl.BlockSpec((1, d), lambda i: (0, 0)),
                  pl.BlockSpec((d, in_w), lambda i: (0, 0))],
        out_specs=[out_spec] * 4,
        out_shape=outs,
        compiler_params=pltpu.CompilerParams(
            dimension_semantics=("arbitrary",), vmem_limit_bytes=VMEM_LIMIT),
        name="proj_sample",
    )(x, g_mix, w_in)


def _conv_kernel(u_ref, hist_ref, cw_ref, cb_ref, lg_ref, lb_ref, a_ref, nc_ref, uh_ref, *, t_len, rows):
    uh_ref[0:HIST_PAD, :] = hist_ref[0]
    uh_ref[HIST_PAD:HIST_PAD + t_len, :] = u_ref[0]
    nc_ref[0] = uh_ref[t_len:t_len + HIST_PAD, :]
    cw = cw_ref[...]
    cb, lg, lb = cb_ref[...], lg_ref[...], lb_ref[...]
    lead = HIST_PAD - CONV_HIST

    def chunk(c, carry):
        r0 = pl.multiple_of(c * rows, 16)
        win = uh_ref[pl.ds(r0, rows + HIST_PAD), :]
        acc = jnp.broadcast_to(cb, (rows, CONV_CH))
        for s in range(8):
            taps = [k for k in range(CONV_WIDTH) if (lead + k) % 8 == s]
            shifted = win if s == 0 else win[s:s + rows + HIST_PAD - 8, :]
            for k in taps:
                off = lead + k - s
                acc = acc + cw[k:k + 1, :] * shifted[off:off + rows, :]
        mu = jnp.mean(acc, axis=-1, keepdims=True)
        dlt = acc - mu
        var = jnp.mean(dlt * dlt, axis=-1, keepdims=True)
        y = dlt * lax.rsqrt(var + EPS) * lg + lb
        a_ref[0, pl.ds(r0, rows), :] = (y * jax.nn.sigmoid(y)).astype(_BF16)
        return carry

    lax.fori_loop(0, t_len // rows, chunk, 0)


def _conv(u, hist_padded, conv_w, conv_b, ln_g, ln_b, rows):
    b, t_len, c = u.shape
    assert t_len % rows == 0 and rows % 16 == 0
    vec = pl.BlockSpec((1, c), lambda i: (0, 0))
    return pl.pallas_call(
        functools.partial(_conv_kernel, t_len=t_len, rows=rows),
        grid=(b,),
        in_specs=[pl.BlockSpec((1, t_len, c), lambda i: (i, 0, 0)),
                  pl.BlockSpec((1, HIST_PAD, c), lambda i: (i, 0, 0)),
                  pl.BlockSpec((CONV_WIDTH, c), lambda i: (0, 0)),
                  vec, vec, vec],
        out_specs=[pl.BlockSpec((1, t_len, c), lambda i: (i, 0, 0)),
                   pl.BlockSpec((1, HIST_PAD, c), lambda i: (i, 0, 0))],
        out_shape=[jax.ShapeDtypeStruct((b, t_len, c), _BF16),
                   jax.ShapeDtypeStruct((b, HIST_PAD, c), _F32)],
        scratch_shapes=[pltpu.VMEM((HIST_PAD + t_len, c), _F32)],
        compiler_params=pltpu.CompilerParams(
            dimension_semantics=("arbitrary",), vmem_limit_bytes=VMEM_LIMIT),
        name=f"conv_t{t_len}",
    )(u, hist_padded, conv_w, conv_b, ln_g, ln_b)


def _sb_kernel(*refs, mq, g_lock, n_super, q_off, front_pad, n_past):
    if n_past:
        q_ref, kn_ref, vn_ref, kp_ref, vp_ref, g_ref, tri_ref, o_ref = refs[:8]
    else:
        q_ref, kn_ref, vn_ref, g_ref, tri_ref, o_ref = refs[:6]
    kb_ref, v0_ref, v1_ref, acc_ref, carry_ref = refs[-5:]
    t_new = kn_ref.shape[1]
    rows = kb_ref.shape[0]
    lane_row = lax.broadcasted_iota(jnp.int32, (1, LANES), 1) < SB_HEAD_DIM

    def fill(r0, n, k, v):
        kb_ref[r0:r0 + n, :] = k.astype(_BF16)
        v0_ref[r0:r0 + n, :] = jnp.where(lane_row, v, 0.0).astype(_BF16)
        v1_ref[r0:r0 + n, :] = jnp.where(lane_row, 0.0, v).astype(_BF16)

    new0 = front_pad + n_past
    if front_pad:
        fill(0, front_pad, jnp.zeros((front_pad, LANES), _F32), jnp.zeros((front_pad, LANES), _F32))
    if n_past:
        fill(front_pad, n_past, kp_ref[0], vp_ref[0])
    fill(new0, t_new, kn_ref[0], vn_ref[0])
    tail = rows - new0 - t_new
    if tail:
        fill(new0 + t_new, tail, jnp.zeros((tail, LANES), _F32), jnp.zeros((tail, LANES), _F32))
    if q_off:
        o_ref[0, 0:q_off, :] = jnp.zeros((q_off, LANES), _BF16)

    tri = tri_ref[...]
    gsb = g_ref[...]
    lane_lo = lax.broadcasted_iota(jnp.int32, (mq, LANES), 1) < SB_HEAD_DIM
    col = lax.broadcasted_iota(jnp.int32, (mq, KEY_CHUNK), 1)
    col_minus_row = col - lax.broadcasted_iota(jnp.int32, (mq, KEY_CHUNK), 0)
    neg_col = -col
    assert (new0 + q_off) % KEY_CHUNK == 0 and (mq == KEY_CHUNK or g_lock * n_super == 1)
    diag0 = (new0 + q_off) // KEY_CHUNK
    vh = (v0_ref, v1_ref)
    nt_dims = (((1,), (1,)), ((), ()))

    def superblock(sb, unused):
        i0 = sb * g_lock
        q0s = [pl.multiple_of(q_off + (i0 + g) * mq, 16) for g in range(g_lock)]
        qhs = []
        for g in range(g_lock):
            qb = q_ref[0, pl.ds(q0s[g], mq), :]
            zero_q = jnp.zeros_like(qb)
            qhs.append((jnp.where(lane_lo, qb, zero_q), jnp.where(lane_lo, zero_q, qb)))
        acc_ref[...] = jnp.zeros_like(acc_ref)
        carry_ref[...] = jnp.zeros_like(carry_ref)

        def sweep(state):
            n = state[0]
            for g in range(g_lock):
                j_raw = diag0 + i0 + g - n
                j = jnp.maximum(j_raw, 0)
                thr = jnp.where(j_raw >= 1, KEY_CHUNK * n,
                                jnp.where(j_raw == 0, 1 - front_pad, -2 * KEY_CHUNK))
                vis = jnp.where(j_raw >= 1, col_minus_row, neg_col) < thr
                r0 = pl.multiple_of(j * KEY_CHUNK, KEY_CHUNK)
                kchunk = kb_ref[pl.ds(r0, KEY_CHUNK), :]
                for h in range(2):
                    z = lax.dot_general(qhs[g][h], kchunk, nt_dims, preferred_element_type=_F32)
                    z = jnp.where(vis, z, MASKED_Z)
                    s = jnp.maximum(z, 0.0) + jnp.log(1.0 + jnp.exp(-jnp.abs(z)))
                    hi = s.astype(_BF16)
                    lo = (s - hi.astype(_F32)).astype(_BF16)
                    ct_r = _dot(jnp.concatenate([hi[:, KEY_BLOCK:], lo[:, KEY_BLOCK:]], axis=1), tri)
                    ct_l = _dot(jnp.concatenate([hi[:, :KEY_BLOCK], lo[:, :KEY_BLOCK]], axis=1), tri)
                    carry = carry_ref[g, h]
                    c_r = ct_r[:, :KEY_BLOCK] + carry
                    carry = carry + ct_r[:, KEY_BLOCK:]
                    c_l = ct_l[:, :KEY_BLOCK] + carry
                    carry_ref[g, h] = carry + ct_l[:, KEY_BLOCK:]
                    w = jnp.exp(z - jnp.concatenate([c_l, c_r], axis=1))
                    acc_ref[g] += _dot(w.astype(_BF16), vh[h][pl.ds(r0, KEY_CHUNK), :])
            more = (diag0 + i0 + g_lock - 1 - n) >= 1
            settled = jnp.min(carry_ref[...]) > EXIT_LOG
            return n + 1, jnp.where(more & jnp.logical_not(settled), 1, 0)

        lax.while_loop(lambda st: st[1] > 0, sweep, (jnp.int32(0), jnp.int32(1)))

        for g in range(g_lock):
            o = acc_ref[g]
            o2 = o * o
            s_all = jnp.sum(o2, axis=-1, keepdims=True)
            s_lo = jnp.sum(jnp.where(lane_lo, o2, 0.0), axis=-1, keepdims=True)
            inv = jnp.where(lane_lo, lax.rsqrt(s_lo / SB_HEAD_DIM + EPS),
                            lax.rsqrt((s_all - s_lo) / SB_HEAD_DIM + EPS))
            o_ref[0, pl.ds(q0s[g], mq), :] = (o * inv * gsb).astype(_BF16)
        return unused

    lax.fori_loop(0, n_super, superblock, 0)


def _tri_matrix():
    r = lax.broadcasted_iota(jnp.int32, (2 * KEY_BLOCK, 2 * KEY_BLOCK), 0) % KEY_BLOCK
    c = lax.broadcasted_iota(jnp.int32, (2 * KEY_BLOCK, 2 * KEY_BLOCK), 1)
    return jnp.where((c >= KEY_BLOCK) | (r >= c), 1.0, 0.0).astype(_BF16)


def _stick_breaking(q, k_new, v_new, k_past, v_past, g_sb, *, mq, g_lock, q_off, front_pad):
    b, t_new, _ = k_new.shape
    n_past = 0 if k_past is None else k_past.shape[1]
    n_super = (t_new - q_off) // (mq * g_lock)
    assert n_super * mq * g_lock == t_new - q_off
    rows = pl.cdiv(front_pad + n_past + t_new, KEY_CHUNK) * KEY_CHUNK
    pairs = SB_WIDTH // LANES
    new_spec = pl.BlockSpec((1, t_new, LANES), lambda i, p: (i, 0, p))
    in_specs = [new_spec, new_spec, new_spec]
    args = [q, k_new, v_new]
    if n_past:
        past_spec = pl.BlockSpec((1, n_past, LANES), lambda i, p: (i, 0, p))
        in_specs += [past_spec, past_spec]
        args += [k_past, v_past]
    in_specs += [pl.BlockSpec((1, LANES), lambda i, p: (0, p)),
                 pl.BlockSpec((2 * KEY_BLOCK, 2 * KEY_BLOCK), lambda i, p: (0, 0))]
    args += [g_sb, _tri_matrix()]
    return pl.pallas_call(
        functools.partial(_sb_kernel, mq=mq, g_lock=g_lock, n_super=n_super, q_off=q_off,
                          front_pad=front_pad, n_past=n_past),
        grid=(b, pairs),
        in_specs=in_specs,
        out_specs=pl.BlockSpec((1, t_new, LANES), lambda i, p: (i, 0, p)),
        out_shape=jax.ShapeDtypeStruct((b, t_new, SB_WIDTH), _BF16),
        scratch_shapes=[pltpu.VMEM((rows, LANES), _BF16)] * 3
        + [pltpu.VMEM((g_lock, mq, LANES), _F32), pltpu.VMEM((g_lock, 2, mq, LANES), _F32)],
        compiler_params=pltpu.CompilerParams(
            dimension_semantics=("arbitrary", "arbitrary"), vmem_limit_bytes=VMEM_LIMIT),
        name=f"stick_breaking_t{t_new}",
    )(*args)


def _mix_kernel(a_ref, b_ref, x_ref, wa_ref, wb_ref, g_ref, wr_ref, br_ref,
                x2_ref, h2_ref, gates_ref, *, n, off):
    r = pl.program_id(1)
    a0 = pl.multiple_of(off + r * n, 16)
    x2 = (x_ref[0] + _dot(a_ref[0, pl.ds(a0, n), :], wa_ref[...])
          + _dot(b_ref[0, pl.ds(a0, n), :], wb_ref[...]))
    x2_ref[0] = x2
    hf = _rms(x2, g_ref[...])
    h2_ref[0] = hf.astype(_BF16)
    logits = jnp.dot(hf, wr_ref[...], preferred_element_type=_F32,
                     precision=lax.Precision.HIGHEST) + br_ref[...]
    lane = lax.broadcasted_iota(jnp.int32, (n, LANES), 1)
    neg = -jnp.inf
    big = 4 * LANES
    is_g = (lane >= N_EXPERTS) & (lane < N_EXPERTS + N_GROUPS)
    lg = jnp.where(is_g, logits, neg)
    g_max = jnp.max(lg, axis=-1, keepdims=True)
    p_g_sel = 1.0 / jnp.sum(jnp.where(is_g, jnp.exp(lg - g_max), 0.0), axis=-1, keepdims=True)
    g_sel = jnp.min(jnp.where(lg == g_max, lane, big), axis=-1, keepdims=True) - N_EXPERTS
    in_grp = (lane < N_EXPERTS) & ((lane // EXPERTS_PER_GROUP) == g_sel)
    le = jnp.where(in_grp, logits, neg)
    e_max = jnp.max(le, axis=-1, keepdims=True)
    pe = jnp.where(in_grp, jnp.exp(le - e_max), 0.0)
    pe = pe / jnp.sum(pe, axis=-1, keepdims=True)
    pe = jnp.where(in_grp, pe, -1.0)
    top1 = jnp.max(pe, axis=-1, keepdims=True)
    i1 = jnp.min(jnp.where(pe == top1, lane, big), axis=-1, keepdims=True)
    pe_rest = jnp.where(lane == i1, -1.0, pe)
    top2 = jnp.max(pe_rest, axis=-1, keepdims=True)
    i2 = jnp.min(jnp.where(pe_rest == top2, lane, big), axis=-1, keepdims=True)
    denom = top1 + top2
    gates_ref[0] = jnp.where(lane == i1, top1 / denom, jnp.where(lane == i2, top2 / denom, 0.0)) * p_g_sel


def _mix(a, b_out, x, wa, wb, g_ffn, wr, br, *, n, off):
    bsz, t_x, d = x.shape
    t_a = a.shape[1]
    assert t_x % n == 0
    res = pl.BlockSpec((1, t_a, CONV_CH), lambda i, r: (i, 0, 0))
    tile = pl.BlockSpec((1, n, d), lambda i, r: (i, r, 0))
    full = lambda shape: pl.BlockSpec(shape, lambda i, r: (0,) * len(shape))
    return pl.pallas_call(
        functools.partial(_mix_kernel, n=n, off=off),
        grid=(bsz, t_x // n),
        in_specs=[res, res, tile, full(wa.shape), full(wb.shape), full((1, d)),
                  full(wr.shape), full((1, LANES))],
        out_specs=[tile, tile, pl.BlockSpec((1, n, LANES), lambda i, r: (i, r, 0))],
        out_shape=[jax.ShapeDtypeStruct((bsz, t_x, d), _F32),
                   jax.ShapeDtypeStruct((bsz, t_x, d), _BF16),
                   jax.ShapeDtypeStruct((bsz, t_x, LANES), _F32)],
        compiler_params=pltpu.CompilerParams(
            dimension_semantics=("arbitrary", "arbitrary"), vmem_limit_bytes=VMEM_LIMIT),
        name=f"mix_t{t_x}",
    )(a, b_out, x, wa, wb, g_ffn, wr, br)


def _moe_kernel(h_ref, gates_ref, x2_ref, win_ref, wout_ref, gf_ref, y_ref, acc_ref):
    e = pl.program_id(1)

    @pl.when(e == 0)
    def _():
        acc_ref[...] = jnp.zeros_like(acc_ref)

    gu = _dot(h_ref[...], win_ref[0])
    act = (jax.nn.silu(gu[:, :D_EXPERT]) * gu[:, D_EXPERT:]).astype(_BF16)
    gates = gates_ref[...]
    lane = lax.broadcasted_iota(jnp.int32, gates.shape, 1)
    gate_e = jnp.sum(jnp.where(lane == e, gates, 0.0), axis=-1, keepdims=True)
    acc_ref[...] += gate_e * _dot(act, wout_ref[0])

    @pl.when(e == pl.num_programs(1) - 1)
    def _():
        y_ref[...] = _rms(x2_ref[...] + acc_ref[...], gf_ref[...])


def _moe(h2, gates, x2, w_e_in, w_e_out, g_final, *, tm):
    rows, d = h2.shape
    assert rows % tm == 0
    n_e = w_e_in.shape[0]
    tok = lambda w: pl.BlockSpec((tm, w), lambda i, e: (i, 0))
    return pl.pallas_call(
        _moe_kernel,
        grid=(rows // tm, n_e),
        in_specs=[tok(d), tok(LANES), tok(d),
                  pl.BlockSpec((1, d, 2 * D_EXPERT), lambda i, e: (e, 0, 0)),
                  pl.BlockSpec((1, D_EXPERT, d), lambda i, e: (e, 0, 0)),
                  pl.BlockSpec((1, d), lambda i, e: (0, 0))],
        out_specs=tok(d),
        out_shape=jax.ShapeDtypeStruct((rows, d), _F32),
        scratch_shapes=[pltpu.VMEM((tm, d), _F32)],
        compiler_params=pltpu.CompilerParams(
            dimension_semantics=("arbitrary", "arbitrary"), vmem_limit_bytes=VMEM_LIMIT),
        name=f"moe_r{rows}",
    )(h2, gates, x2, w_e_in, w_e_out, g_final)


def kernel(x_prompt, x_sample, cache_k, cache_v, cache_conv, meta_tokens, g_mix, w_in, conv_w, conv_b,
           conv_ln_g, conv_ln_b, g_sb, w_out, g_ffn, w_router_group, b_router_group, w_router_expert,
           b_router_expert, w_expert_in, w_expert_out, g_final):
    depth = w_in.shape[0]
    assert depth == 1, "single-layer trunk"
    bp, tp, d = x_prompt.shape
    bs, ts, _ = x_sample.shape
    n_past = cache_k.shape[2]

    w_in_b = w_in[0].astype(_BF16)
    wa = w_out[0, :CONV_CH].astype(_BF16)
    wb = w_out[0, CONV_CH:].astype(_BF16)
    w_e_in = w_expert_in[0].astype(_BF16)
    w_e_out = w_expert_out[0].astype(_BF16)
    pad = LANES - N_EXPERTS - N_GROUPS
    wr = jnp.concatenate([w_router_expert[0], w_router_group[0], jnp.zeros((d, pad), _F32)], axis=1)
    br = jnp.concatenate([b_router_expert[0], b_router_group[0], jnp.zeros((pad,), _F32)])[None]
    g_mix2, g_ffn2, g_fin2, g_sb2 = g_mix[0][None], g_ffn[0][None], g_final[None], g_sb[0][None]
    cb, lg, lb = conv_b[0][None], conv_ln_g[0][None], conv_ln_b[0][None]

    def tail(a, b_out, x, *, n, off, tm):
        x2, h2, gates = _mix(a, b_out, x, wa, wb, g_ffn2, wr, br, n=n, off=off)
        rows = x.shape[0] * x.shape[1]
        y = _moe(h2.reshape(rows, d), gates.reshape(rows, LANES), x2.reshape(rows, d),
                 w_e_in, w_e_out, g_fin2, tm=tm)
        return y.reshape(x.shape)

    u_p, q_p, k_p, v_p = _proj_prompt(x_prompt, meta_tokens, g_mix2, w_in_b)
    a_p, nc_p = _conv(u_p, jnp.zeros((bp, HIST_PAD, CONV_CH), _F32), conv_w[0], cb, lg, lb, rows=48)
    b_p = _stick_breaking(q_p, k_p, v_p, None, None, g_sb2, mq=KEY_CHUNK, g_lock=2, q_off=N_META,
                          front_pad=KEY_CHUNK - N_META)
    y_p = tail(a_p, b_p, x_prompt, n=512, off=N_META, tm=1024)

    u_s, q_s, k_s, v_s = _proj_sample(x_sample, g_mix2, w_in_b)
    hist = jnp.pad(cache_conv[0], ((0, 0), (HIST_PAD - CONV_HIST, 0), (0, 0)))
    a_s, nc_s = _conv(u_s, hist, conv_w[0], cb, lg, lb, rows=ts)
    b_s = _stick_breaking(q_s, k_s, v_s, cache_k[0].reshape(bs, n_past, SB_WIDTH),
                          cache_v[0].reshape(bs, n_past, SB_WIDTH), g_sb2, mq=ts, g_lock=1, q_off=0,
                          front_pad=0)
    y_s = tail(a_s, b_s, x_sample, n=ts, off=0, tm=bs * ts)

    heads = (SB_HEADS, SB_HEAD_DIM)
    lead = HIST_PAD - CONV_HIST
    return (y_p, y_s,
            k_p.reshape(1, bp, tp + N_META, *heads), v_p.reshape(1, bp, tp + N_META, *heads),
            nc_p[None, :, lead:],
            k_s.reshape(1, bs, ts, *heads), v_s.reshape(1, bs, ts, *heads),
            nc_s[None, :, lead:])
```

```python
import functools

import jax
import jax.numpy as jnp
from jax import lax
from jax.experimental import pallas as pl
from jax.experimental.pallas import tpu as pltpu

D_MODEL = 1024
N_META = 16
CONV_CH = 512
CONV_WIDTH = 31
CONV_HIST = CONV_WIDTH - 1
SB_HEADS = 8
SB_HEAD_DIM = 64
SB_WIDTH = SB_HEADS * SB_HEAD_DIM
N_GROUPS = 4
EXPERTS_PER_GROUP = 8
N_EXPERTS = N_GROUPS * EXPERTS_PER_GROUP
D_EXPERT = 256
EPS = 1e-6

LANES = 128
HIST_PAD = 32
KEY_BLOCK = 128
KEY_CHUNK = 256
LOG2_E = 1.4426950408889634
MASKED_Z = -1e30
EXIT_LOG2 = 128.0
ROUTER_ROWS = 40
GROUP_ROW = N_EXPERTS
ROW_BLOCK = 128
PERM_ROWS = 256
EXPERT_STEP = 4
VMEM_LIMIT = 52 * 1024 * 1024
MOE_VMEM_LIMIT = 58 * 1024 * 1024

_F32 = jnp.float32
_BF16 = jnp.bfloat16


def _dot(a, b):
    return jnp.dot(a, b, preferred_element_type=_F32)


def _rms(x, g):
    return x * lax.rsqrt(jnp.mean(x * x, axis=-1, keepdims=True) + EPS) * g


def _proj_store(h, w_ref, u_ref, q_ref, k_ref, v_ref):
    c, s = CONV_CH, SB_WIDTH
    val = _dot(h, w_ref[:, 0:c])
    gate = _dot(h, w_ref[:, c:2 * c])
    u_ref[0] = val * jax.nn.sigmoid(gate)
    q_ref[0] = (_dot(h, w_ref[:, 2 * c:2 * c + s]) * (SB_HEAD_DIM ** -0.5 * LOG2_E)).astype(_BF16)
    k_ref[0] = _dot(h, w_ref[:, 2 * c + s:2 * c + 2 * s])
    v_ref[0] = _dot(h, w_ref[:, 2 * c + 2 * s:2 * c + 3 * s])


def _proj_prompt_kernel(x_ref, meta_ref, g_ref, w_ref, u_ref, q_ref, k_ref, v_ref, h_ref, *, tile):
    t = pl.program_id(1)
    g = g_ref[...]
    main0 = pl.multiple_of(t * tile, 8)
    h_ref[N_META:, :] = _rms(x_ref[0, pl.ds(main0, tile - N_META), :], g).astype(_BF16)
    head0 = pl.multiple_of(jnp.maximum(t * tile - N_META, 0), 8)
    xh = jnp.where(t == 0, meta_ref[...], x_ref[0, pl.ds(head0, N_META), :])
    h_ref[0:N_META, :] = _rms(xh, g).astype(_BF16)
    _proj_store(h_ref[...], w_ref, u_ref, q_ref, k_ref, v_ref)


def _proj_sample_kernel(x_ref, g_ref, w_ref, u_ref, q_ref, k_ref, v_ref):
    _proj_store(_rms(x_ref[0], g_ref[...]).astype(_BF16), w_ref, u_ref, q_ref, k_ref, v_ref)


def _proj_prompt(x, meta, g_mix, w_in):
    b, t_x, d = x.shape
    t_all = t_x + N_META
    tile = t_all // 3
    assert tile * 3 == t_all and tile % 16 == 0
    in_w = w_in.shape[1]
    outs = [jax.ShapeDtypeStruct((b, t_all, CONV_CH), _F32),
            jax.ShapeDtypeStruct((b, t_all, SB_WIDTH), _BF16),
            jax.ShapeDtypeStruct((b, t_all, SB_WIDTH), _F32),
            jax.ShapeDtypeStruct((b, t_all, SB_WIDTH), _F32)]
    out_spec = pl.BlockSpec((1, tile, CONV_CH), lambda i, j: (i, j, 0))
    return pl.pallas_call(
        functools.partial(_proj_prompt_kernel, tile=tile),
        grid=(b, 3),
        in_specs=[pl.BlockSpec((1, t_x, d), lambda i, j: (i, 0, 0)),
                  pl.BlockSpec((N_META, d), lambda i, j: (0, 0)),
                  pl.BlockSpec((1, d), lambda i, j: (0, 0)),
                  pl.BlockSpec((d, in_w), lambda i, j: (0, 0))],
        out_specs=[out_spec] * 4,
        out_shape=outs,
        scratch_shapes=[pltpu.VMEM((tile, d), _BF16)],
        compiler_params=pltpu.CompilerParams(
            dimension_semantics=("arbitrary", "arbitrary"), vmem_limit_bytes=VMEM_LIMIT),
        name="proj_prompt",
    )(x, meta, g_mix, w_in)


def _proj_sample(x, g_mix, w_in):
    b, t, d = x.shape
    in_w = w_in.shape[1]
    outs = [jax.ShapeDtypeStruct((b, t, CONV_CH), _F32),
            jax.ShapeDtypeStruct((b, t, SB_WIDTH), _BF16),
            jax.ShapeDtypeStruct((b, t, SB_WIDTH), _F32),
            jax.ShapeDtypeStruct((b, t, SB_WIDTH), _F32)]
    out_spec = pl.BlockSpec((1, t, CONV_CH), lambda i: (i, 0, 0))
    return pl.pallas_call(
        _proj_sample_kernel,
        grid=(b,),
        in_specs=[pl.BlockSpec((1, t, d), lambda i: (i, 0, 0)),
                  pl.BlockSpec((1, d), lambda i: (0, 0)),
                  pl.BlockSpec((d, in_w), lambda i: (0, 0))],
        out_specs=[out_spec] * 4,
        out_shape=outs,
        compiler_params=pltpu.CompilerParams(
            dimension_semantics=("arbitrary",), vmem_limit_bytes=VMEM_LIMIT),
        name="proj_sample",
    )(x, g_mix, w_in)


def _conv_kernel(u_ref, hist_ref, cw_ref, cb_ref, lg_ref, lb_ref, a_ref, nc_ref, uh_ref, cwb_ref,
                 *, t_len, t0, rows):
    uh_ref[0:HIST_PAD, :] = hist_ref[0]
    uh_ref[HIST_PAD:HIST_PAD + t_len, :] = u_ref[0]
    nc_ref[0] = uh_ref[t_len:t_len + HIST_PAD, :]
    if t0:
        a_ref[0, 0:t0, :] = jnp.zeros((t0, CONV_CH), _BF16)
    for k in range(CONV_WIDTH):
        cwb_ref[8 * k:8 * k + 8, :] = jnp.broadcast_to(cw_ref[k:k + 1, :], (8, CONV_CH))
    cb, lg, lb = cb_ref[...], lg_ref[...], lb_ref[...]
    lead = HIST_PAD - CONV_HIST
    tiles = rows // 8

    def chunk(c, carry):
        r0 = pl.multiple_of(t0 + c * rows, 16)
        win = uh_ref[pl.ds(r0, rows + HIST_PAD), :]
        acc = jnp.zeros((tiles, 8, CONV_CH), _F32)
        for s in range(8):
            taps = [k for k in range(CONV_WIDTH) if (lead + k) % 8 == s]
            shifted = win if s == 0 else win[s:s + rows + HIST_PAD - 8, :]
            for k in taps:
                off = lead + k - s
                tap = shifted[off:off + rows, :].reshape(tiles, 8, CONV_CH)
                acc = acc + cwb_ref[8 * k:8 * k + 8, :][None] * tap
        acc = acc.reshape(rows, CONV_CH) + cb
        mu = jnp.mean(acc, axis=-1, keepdims=True)
        dlt = acc - mu
        var = jnp.mean(dlt * dlt, axis=-1, keepdims=True)
        y = dlt * lax.rsqrt(var + EPS) * lg + lb
        a_ref[0, pl.ds(r0, rows), :] = (y * jax.nn.sigmoid(y)).astype(_BF16)
        return carry

    lax.fori_loop(0, (t_len - t0) // rows, chunk, 0)


def _conv(u, hist_padded, conv_w, conv_b, ln_g, ln_b, *, t0, rows):
    b, t_len, c = u.shape
    assert (t_len - t0) % rows == 0 and rows % 16 == 0 and t0 % 16 == 0
    vec = pl.BlockSpec((1, c), lambda i: (0, 0))
    return pl.pallas_call(
        functools.partial(_conv_kernel, t_len=t_len, t0=t0, rows=rows),
        grid=(b,),
        in_specs=[pl.BlockSpec((1, t_len, c), lambda i: (i, 0, 0)),
                  pl.BlockSpec((1, HIST_PAD, c), lambda i: (i, 0, 0)),
                  pl.BlockSpec((CONV_WIDTH, c), lambda i: (0, 0)),
                  vec, vec, vec],
        out_specs=[pl.BlockSpec((1, t_len, c), lambda i: (i, 0, 0)),
                   pl.BlockSpec((1, HIST_PAD, c), lambda i: (i, 0, 0))],
        out_shape=[jax.ShapeDtypeStruct((b, t_len, c), _BF16),
                   jax.ShapeDtypeStruct((b, HIST_PAD, c), _F32)],
        scratch_shapes=[pltpu.VMEM((HIST_PAD + t_len, c), _F32), pltpu.VMEM((8 * CONV_WIDTH, c), _F32)],
        compiler_params=pltpu.CompilerParams(
            dimension_semantics=("arbitrary",), vmem_limit_bytes=VMEM_LIMIT),
        name=f"conv_t{t_len}",
    )(u, hist_padded, conv_w, conv_b, ln_g, ln_b)


def _sb_kernel(*refs, mq, g_lock, n_super, q_off, front_pad, n_past):
    if n_past:
        q_ref, kn_ref, vn_ref, kp_ref, vp_ref, g_ref, tri_ref, o_ref = refs[:8]
    else:
        q_ref, kn_ref, vn_ref, g_ref, tri_ref, o_ref = refs[:6]
    kb_ref, v0_ref, v1_ref, acc_ref, carry_ref = refs[-5:]
    t_new = kn_ref.shape[1]
    rows = kb_ref.shape[0]
    lane_row = lax.broadcasted_iota(jnp.int32, (1, LANES), 1) < SB_HEAD_DIM

    def fill(r0, n, k, v):
        kb_ref[r0:r0 + n, :] = k.astype(_BF16)
        v0_ref[r0:r0 + n, :] = jnp.where(lane_row, v, 0.0).astype(_BF16)
        v1_ref[r0:r0 + n, :] = jnp.where(lane_row, 0.0, v).astype(_BF16)

    new0 = front_pad + n_past
    if front_pad:
        fill(0, front_pad, jnp.zeros((front_pad, LANES), _F32), jnp.zeros((front_pad, LANES), _F32))
    if n_past:
        fill(front_pad, n_past, kp_ref[0], vp_ref[0])
    fill(new0, t_new, kn_ref[0], vn_ref[0])
    tail = rows - new0 - t_new
    if tail:
        fill(new0 + t_new, tail, jnp.zeros((tail, LANES), _F32), jnp.zeros((tail, LANES), _F32))
    if q_off:
        o_ref[0, 0:q_off, :] = jnp.zeros((q_off, LANES), _BF16)

    tri = tri_ref[...]
    gsb = g_ref[...]
    lane_lo = lax.broadcasted_iota(jnp.int32, (mq, LANES), 1) < SB_HEAD_DIM
    col_minus_row = (lax.broadcasted_iota(jnp.int32, (2 * mq, KEY_BLOCK), 1) + KEY_BLOCK
                     - lax.broadcasted_iota(jnp.int32, (2 * mq, KEY_BLOCK), 0) % mq)
    assert (new0 + q_off) % KEY_BLOCK == 0 and front_pad >= KEY_BLOCK
    assert mq == KEY_BLOCK or g_lock * n_super == 1
    end0 = new0 + q_off + KEY_BLOCK
    nt_dims = (((1,), (1,)), ((), ()))

    def superblock(sb, unused):
        i0 = sb * g_lock
        q0s = [pl.multiple_of(q_off + (i0 + g) * mq, 16) for g in range(g_lock)]
        qhs = []
        for g in range(g_lock):
            qb = q_ref[0, pl.ds(q0s[g], mq), :]
            zero_q = jnp.zeros_like(qb)
            qhs.append(jnp.concatenate([jnp.where(lane_lo, qb, zero_q), jnp.where(lane_lo, zero_q, qb)],
                                       axis=0))
        acc_ref[...] = jnp.zeros_like(acc_ref)
        carry_ref[...] = jnp.zeros_like(carry_ref)

        def sweep(state):
            n = state[0]
            low = jnp.full((1, LANES), jnp.inf, _F32)
            starts = [end0 + (i0 + g) * mq - (n + 1) * KEY_CHUNK for g in range(g_lock)]
            r0s = [pl.multiple_of(jnp.maximum(st, 0), KEY_BLOCK) for st in starts]
            zs, cts = [], []
            for g in range(g_lock):
                z = lax.dot_general(qhs[g], kb_ref[pl.ds(r0s[g], KEY_CHUNK), :], nt_dims,
                                    preferred_element_type=_F32)
                hidden_r = jnp.where(col_minus_row < KEY_BLOCK + n * KEY_CHUNK, z[:, KEY_BLOCK:], MASKED_Z)
                zs.append(jnp.concatenate([z[:, :KEY_BLOCK], hidden_r], axis=1))
            for g in range(g_lock):
                z = zs[g]
                s = jnp.maximum(z, 0.0) + jnp.log2(1.0 + jnp.exp2(-jnp.abs(z)))
                hi = s.astype(_BF16)
                lo = (s - hi.astype(_F32)).astype(_BF16)
                cts.append(_dot(jnp.concatenate(
                    [jnp.concatenate([hi[:, KEY_BLOCK:], lo[:, KEY_BLOCK:]], axis=1),
                     jnp.concatenate([hi[:, :KEY_BLOCK], lo[:, :KEY_BLOCK]], axis=1)], axis=0), tri))
            for g in range(g_lock):
                ct_r, ct_l = cts[g][:2 * mq], cts[g][2 * mq:]
                carry = carry_ref[g]
                c_r = ct_r[:, :KEY_BLOCK] + carry
                carry = carry + ct_r[:, KEY_BLOCK:]
                c_l = ct_l[:, :KEY_BLOCK] + carry
                carry = carry + ct_l[:, KEY_BLOCK:]
                carry_ref[g] = carry
                w = jnp.exp2(zs[g] - jnp.concatenate([c_l, c_r], axis=1)).astype(_BF16)
                pv = (_dot(w[:mq], v0_ref[pl.ds(r0s[g], KEY_CHUNK), :])
                      + _dot(w[mq:], v1_ref[pl.ds(r0s[g], KEY_CHUNK), :]))
                acc_ref[g] += jnp.where(starts[g] >= 0, pv, 0.0)
                done = jnp.where(starts[g] - KEY_CHUNK >= 0, 0.0, 2 * EXIT_LOG2)
                low = jnp.minimum(low, jnp.min(carry, axis=0, keepdims=True) + done)
            return n + 1, jnp.where(jnp.min(low) > EXIT_LOG2, 0, 1)

        lax.while_loop(lambda st: st[1] > 0, sweep, (jnp.int32(0), jnp.int32(1)))

        for g in range(g_lock):
            o = acc_ref[g]
            o2 = o * o
            s_all = jnp.sum(o2, axis=-1, keepdims=True)
            s_lo = jnp.sum(jnp.where(lane_lo, o2, 0.0), axis=-1, keepdims=True)
            inv = jnp.where(lane_lo, lax.rsqrt(s_lo / SB_HEAD_DIM + EPS),
                            lax.rsqrt((s_all - s_lo) / SB_HEAD_DIM + EPS))
            o_ref[0, pl.ds(q0s[g], mq), :] = (o * inv * gsb).astype(_BF16)
        return unused

    lax.fori_loop(0, n_super, superblock, 0)


def _tri_matrix():
    r = lax.broadcasted_iota(jnp.int32, (2 * KEY_BLOCK, 2 * KEY_BLOCK), 0) % KEY_BLOCK
    c = lax.broadcasted_iota(jnp.int32, (2 * KEY_BLOCK, 2 * KEY_BLOCK), 1)
    return jnp.where((c >= KEY_BLOCK) | (r >= c), 1.0, 0.0).astype(_BF16)


def _stick_breaking(q, k_new, v_new, k_past, v_past, g_sb, *, mq, g_lock, q_off, front_pad):
    b, t_new, _ = k_new.shape
    n_past = 0 if k_past is None else k_past.shape[1]
    n_super = (t_new - q_off) // (mq * g_lock)
    assert n_super * mq * g_lock == t_new - q_off
    last_end = front_pad + n_past + q_off + KEY_BLOCK + (n_super * g_lock - 1) * mq
    rows = max(pl.cdiv(front_pad + n_past + t_new, KEY_BLOCK) * KEY_BLOCK, last_end)
    pairs = SB_WIDTH // LANES
    new_spec = pl.BlockSpec((1, t_new, LANES), lambda i, p: (i, 0, p))
    in_specs = [new_spec, new_spec, new_spec]
    args = [q, k_new, v_new]
    if n_past:
        past_spec = pl.BlockSpec((1, n_past, LANES), lambda i, p: (i, 0, p))
        in_specs += [past_spec, past_spec]
        args += [k_past, v_past]
    in_specs += [pl.BlockSpec((1, LANES), lambda i, p: (0, p)),
                 pl.BlockSpec((2 * KEY_BLOCK, 2 * KEY_BLOCK), lambda i, p: (0, 0))]
    args += [g_sb, _tri_matrix()]
    return pl.pallas_call(
        functools.partial(_sb_kernel, mq=mq, g_lock=g_lock, n_super=n_super, q_off=q_off,
                          front_pad=front_pad, n_past=n_past),
        grid=(b, pairs),
        in_specs=in_specs,
        out_specs=pl.BlockSpec((1, t_new, LANES), lambda i, p: (i, 0, p)),
        out_shape=jax.ShapeDtypeStruct((b, t_new, SB_WIDTH), _BF16),
        scratch_shapes=[pltpu.VMEM((rows, LANES), _BF16)] * 3
        + [pltpu.VMEM((g_lock, mq, LANES), _F32), pltpu.VMEM((g_lock, 2 * mq, LANES), _F32)],
        compiler_params=pltpu.CompilerParams(
            dimension_semantics=("arbitrary", "arbitrary"), vmem_limit_bytes=VMEM_LIMIT),
        name=f"stick_breaking_t{t_new}",
    )(*args)


def _split_bf16(x):
    hi = x.astype(_BF16)
    return hi, (x - hi.astype(_F32)).astype(_BF16)


def _mix_kernel(a_ref, b_ref, x_ref, wa_ref, wb_ref, g_ref, wrh_ref, wrl_ref, br_ref,
                x2_ref, h2_ref, gates_ref, *, n, off):
    r = pl.program_id(1)
    a0 = pl.multiple_of(off + r * n, 16)
    x2 = (x_ref[...] + _dot(a_ref[0, pl.ds(a0, n), :], wa_ref[...])
          + _dot(b_ref[0, pl.ds(a0, n), :], wb_ref[...]))
    x2_ref[...] = x2
    hf = _rms(x2, g_ref[...])
    h_hi, h_lo = _split_bf16(hf)
    h2_ref[...] = h_hi
    nt = (((1,), (1,)), ((), ()))
    wrh = wrh_ref[...]
    logits = (lax.dot_general(wrh, h_hi, nt, preferred_element_type=_F32)
              + lax.dot_general(wrh, h_lo, nt, preferred_element_type=_F32)
              + lax.dot_general(wrl_ref[...], h_hi, nt, preferred_element_type=_F32))
    logits = logits[:ROUTER_ROWS] + jnp.tile(br_ref[...], (1, n // LANES))[:ROUTER_ROWS]
    row = lax.broadcasted_iota(jnp.int32, (ROUTER_ROWS, n), 0)
    neg = -jnp.inf
    big = 4 * LANES
    is_g = (row >= N_EXPERTS) & (row < N_EXPERTS + N_GROUPS)
    lg = jnp.where(is_g, logits, neg)
    g_max = jnp.max(lg, axis=0, keepdims=True)
    p_g_sel = 1.0 / jnp.sum(jnp.where(is_g, jnp.exp(lg - g_max), 0.0), axis=0, keepdims=True)
    g_sel = jnp.min(jnp.where(lg == g_max, row, big), axis=0, keepdims=True) - N_EXPERTS
    in_grp = (row < N_EXPERTS) & ((row // EXPERTS_PER_GROUP) == g_sel)
    le = jnp.where(in_grp, logits, neg)
    e_max = jnp.max(le, axis=0, keepdims=True)
    pe = jnp.where(in_grp, jnp.exp(le - e_max), 0.0)
    pe = pe / jnp.sum(pe, axis=0, keepdims=True)
    pe = jnp.where(in_grp, pe, -1.0)
    top1 = jnp.max(pe, axis=0, keepdims=True)
    i1 = jnp.min(jnp.where(pe == top1, row, big), axis=0, keepdims=True)
    pe_rest = jnp.where(row == i1, -1.0, pe)
    top2 = jnp.max(pe_rest, axis=0, keepdims=True)
    i2 = jnp.min(jnp.where(pe_rest == top2, row, big), axis=0, keepdims=True)
    denom = top1 + top2
    gates = jnp.where(row == i1, top1 / denom, jnp.where(row == i2, top2 / denom, 0.0)) * p_g_sel
    gates = jnp.where(row == GROUP_ROW, g_sel.astype(_F32), gates)
    gates_ref[...] = jnp.concatenate([gates, jnp.zeros((LANES - ROUTER_ROWS, n), _F32)], axis=0)


def _mix(a, b_out, x, wa, wb, g_ffn, wr_hi, wr_lo, br, *, n, off):
    bsz, t_a, _ = a.shape
    rows, d = x.shape
    t_x = rows // bsz
    assert t_x % n == 0 and n % LANES == 0
    steps = t_x // n
    res = pl.BlockSpec((1, t_a, CONV_CH), lambda i, r: (i, 0, 0))
    tile = pl.BlockSpec((n, d), lambda i, r: (i * steps + r, 0))
    full = lambda shape: pl.BlockSpec(shape, lambda i, r: (0,) * len(shape))
    return pl.pallas_call(
        functools.partial(_mix_kernel, n=n, off=off),
        grid=(bsz, steps),
        in_specs=[res, res, tile, full(wa.shape), full(wb.shape), full((1, d)),
                  full(wr_hi.shape), full(wr_lo.shape), full((LANES, LANES))],
        out_specs=[tile, tile, pl.BlockSpec((LANES, n), lambda i, r: (0, i * steps + r))],
        out_shape=[jax.ShapeDtypeStruct((rows, d), _F32),
                   jax.ShapeDtypeStruct((rows, d), _BF16),
                   jax.ShapeDtypeStruct((LANES, rows), _F32)],
        compiler_params=pltpu.CompilerParams(
            dimension_semantics=("arbitrary", "arbitrary"), vmem_limit_bytes=VMEM_LIMIT),
        name=f"mix_r{rows}",
    )(a, b_out, x, wa, wb, g_ffn, wr_hi, wr_lo, br)


def _moe_kernel(h_ref, gt_ref, x2_ref, win_ref, wout_ref, gf_ref, y_ref,
                hs_ref, gs_ref, ys_ref, ut_ref, pos_ref, meta_ref, *, tm, tms):
    i = pl.program_id(0)
    s = pl.program_id(1)
    steps_per_group = EXPERTS_PER_GROUP // EXPERT_STEP
    nt = (((1,), (1,)), ((), ()))

    @pl.when((i == 0) & (s == 0))
    def _():
        ut_ref[...] = jnp.where(lax.broadcasted_iota(jnp.int32, (tm, tm), 0)
                                < lax.broadcasted_iota(jnp.int32, (tm, tm), 1), 1.0, 0.0).astype(_BF16)

    @pl.when(s == 0)
    def _():
        gt = gt_ref[...]
        gid = gt[GROUP_ROW:GROUP_ROW + 1, :].astype(jnp.int32)
        grow = lax.broadcasted_iota(jnp.int32, (8, tm), 0)
        onehot = jnp.where(grow == gid, 1.0, 0.0)
        rank = _dot(onehot.astype(_BF16), ut_ref[...])
        off = jnp.int32(0)
        off_rows = jnp.zeros((8, tm), _F32)
        for g in range(N_GROUPS):
            cnt = jnp.sum(jnp.where(grow == g, onehot, 0.0)).astype(jnp.int32)
            blocks = (cnt + ROW_BLOCK - 1) // ROW_BLOCK
            meta_ref[g] = off
            meta_ref[N_GROUPS + g] = blocks
            off_rows = jnp.where(grow == g, off.astype(_F32), off_rows)
            off = off + blocks * ROW_BLOCK
        pos = jnp.sum(onehot * (rank + off_rows), axis=0, keepdims=True)
        pos_ref[...] = jnp.transpose(jnp.broadcast_to(pos, (LANES, tm)))
        pos_i = pos.astype(jnp.int32)
        g_hi, g_lo = _split_bf16(gt)
        h = h_ref[...]
        for c in range(tms // PERM_ROWS):
            prow = lax.broadcasted_iota(jnp.int32, (PERM_ROWS, tm), 0) + c * PERM_ROWS
            perm = jnp.where(prow == pos_i, 1.0, 0.0).astype(_BF16)
            rows = pl.ds(c * PERM_ROWS, PERM_ROWS)
            hs_ref[rows, :] = _dot(perm, h).astype(_BF16)
            gs_ref[rows, :] = (lax.dot_general(perm, g_hi, nt, preferred_element_type=_F32)
                               + lax.dot_general(perm, g_lo, nt, preferred_element_type=_F32))
        ys_ref[...] = jnp.zeros_like(ys_ref)

    grp = s // steps_per_group
    off = meta_ref[grp]
    blocks = meta_ref[N_GROUPS + grp]

    def ffn(r0, m):
        rows = pl.ds(pl.multiple_of(r0, ROW_BLOCK), m)
        hb = hs_ref[rows, :]
        gsb = gs_ref[rows, :]
        lane = lax.broadcasted_iota(jnp.int32, (m, LANES), 1)
        acc = jnp.zeros((m, D_MODEL), _F32)
        for j in range(EXPERT_STEP):
            gu = _dot(hb, win_ref[j])
            act = (jax.nn.silu(gu[:, :D_EXPERT]) * gu[:, D_EXPERT:]).astype(_BF16)
            gate = jnp.sum(jnp.where(lane == s * EXPERT_STEP + j, gsb, 0.0), axis=-1, keepdims=True)
            acc = acc + gate * _dot(act, wout_ref[j])
        ys_ref[rows, :] += acc

    def pair(b, carry):
        ffn(off + b * (2 * ROW_BLOCK), 2 * ROW_BLOCK)
        return carry

    lax.fori_loop(0, blocks // 2, pair, 0)

    @pl.when(blocks % 2 == 1)
    def _():
        ffn(off + (blocks - 1) * ROW_BLOCK, ROW_BLOCK)

    @pl.when(s == pl.num_programs(1) - 1)
    def _():
        hs_ref[...] = ys_ref[...].astype(_BF16)
        for c in range(tm // PERM_ROWS):
            rows = pl.ds(c * PERM_ROWS, PERM_ROWS)
            pos_c = jnp.tile(pos_ref[rows, :], (1, tms // LANES)).astype(jnp.int32)
            unperm = jnp.where(lax.broadcasted_iota(jnp.int32, (PERM_ROWS, tms), 1) == pos_c,
                               1.0, 0.0).astype(_BF16)
            y_ref[rows, :] = _rms(x2_ref[rows, :] + _dot(unperm, hs_ref[...]), gf_ref[...])


def _moe(h2, gates_t, x2, w_e_in, w_e_out, g_final, *, tm):
    rows, d = h2.shape
    assert rows % tm == 0 and tm % PERM_ROWS == 0
    tms = tm + N_GROUPS * ROW_BLOCK
    n_steps = N_EXPERTS // EXPERT_STEP
    tok = lambda w: pl.BlockSpec((tm, w), lambda i, s: (i, 0))
    return pl.pallas_call(
        functools.partial(_moe_kernel, tm=tm, tms=tms),
        grid=(rows // tm, n_steps),
        in_specs=[tok(d), pl.BlockSpec((LANES, tm), lambda i, s: (0, i)), tok(d),
                  pl.BlockSpec((EXPERT_STEP, d, 2 * D_EXPERT), lambda i, s: (s, 0, 0)),
                  pl.BlockSpec((EXPERT_STEP, D_EXPERT, d), lambda i, s: (s, 0, 0)),
                  pl.BlockSpec((1, d), lambda i, s: (0, 0))],
        out_specs=tok(d),
        out_shape=jax.ShapeDtypeStruct((rows, d), _F32),
        scratch_shapes=[pltpu.VMEM((tms, d), _BF16),
                        pltpu.VMEM((tms, LANES), _F32),
                        pltpu.VMEM((tms, d), _F32),
                        pltpu.VMEM((tm, tm), _BF16),
                        pltpu.VMEM((tm, LANES), _F32),
                        pltpu.SMEM((2 * N_GROUPS,), jnp.int32)],
        compiler_params=pltpu.CompilerParams(
            dimension_semantics=("arbitrary", "arbitrary"), vmem_limit_bytes=MOE_VMEM_LIMIT),
        name=f"moe_r{rows}",
    )(h2, gates_t, x2, w_e_in, w_e_out, g_final)


def kernel(x_prompt, x_sample, cache_k, cache_v, cache_conv, meta_tokens, g_mix, w_in, conv_w, conv_b,
           conv_ln_g, conv_ln_b, g_sb, w_out, g_ffn, w_router_group, b_router_group, w_router_expert,
           b_router_expert, w_expert_in, w_expert_out, g_final):
    depth = w_in.shape[0]
    assert depth == 1, "single-layer trunk"
    bp, tp, d = x_prompt.shape
    bs, ts, _ = x_sample.shape
    n_past = cache_k.shape[2]

    w_in_b = w_in[0].astype(_BF16)
    wa = w_out[0, :CONV_CH].astype(_BF16)
    wb = w_out[0, CONV_CH:].astype(_BF16)
    w_e_in = w_expert_in[0].astype(_BF16)
    w_e_out = w_expert_out[0].astype(_BF16)
    pad = LANES - N_EXPERTS - N_GROUPS
    wr = jnp.concatenate([w_router_expert[0].T, w_router_group[0].T, jnp.zeros((pad, d), _F32)], axis=0)
    wr_hi = wr.astype(_BF16)
    wr_lo = (wr - wr_hi.astype(_F32)).astype(_BF16)
    br = jnp.concatenate([b_router_expert[0], b_router_group[0], jnp.zeros((pad,), _F32)])
    br = jnp.broadcast_to(br[:, None], (LANES, LANES))
    g_mix2, g_ffn2, g_fin2, g_sb2 = g_mix[0][None], g_ffn[0][None], g_final[None], g_sb[0][None]
    cb, lg, lb = conv_b[0][None], conv_ln_g[0][None], conv_ln_b[0][None]

    def tail(a, b_out, x, *, n, off, tm):
        rows = x.shape[0] * x.shape[1]
        x2, h2, gates_t = _mix(a, b_out, x.reshape(rows, d), wa, wb, g_ffn2, wr_hi, wr_lo, br, n=n, off=off)
        return _moe(h2, gates_t, x2, w_e_in, w_e_out, g_fin2, tm=tm).reshape(x.shape)

    u_p, q_p, k_p, v_p = _proj_prompt(x_prompt, meta_tokens, g_mix2, w_in_b)
    a_p, nc_p = _conv(u_p, jnp.zeros((bp, HIST_PAD, CONV_CH), _F32), conv_w[0], cb, lg, lb,
                      t0=N_META, rows=256)
    b_p = _stick_breaking(q_p, k_p, v_p, None, None, g_sb2, mq=KEY_BLOCK, g_lock=8, q_off=N_META,
                          front_pad=2 * KEY_BLOCK - N_META)
    y_p = tail(a_p, b_p, x_prompt, n=512, off=N_META, tm=1024)

    u_s, q_s, k_s, v_s = _proj_sample(x_sample, g_mix2, w_in_b)
    hist = jnp.pad(cache_conv[0], ((0, 0), (HIST_PAD - CONV_HIST, 0), (0, 0)))
    a_s, nc_s = _conv(u_s, hist, conv_w[0], cb, lg, lb, t0=0, rows=ts)
    b_s = _stick_breaking(q_s, k_s, v_s, cache_k[0].reshape(bs, n_past, SB_WIDTH),
                          cache_v[0].reshape(bs, n_past, SB_WIDTH), g_sb2, mq=ts, g_lock=1, q_off=0,
                          front_pad=KEY_BLOCK)
    flat = lambda t: t.reshape(1, bs * ts, t.shape[-1])
    y_s = tail(flat(a_s), flat(b_s), flat(x_sample), n=bs * ts, off=0, tm=bs * ts).reshape(x_sample.shape)

    heads = (SB_HEADS, SB_HEAD_DIM)
    lead = HIST_PAD - CONV_HIST
    return (y_p, y_s,
            k_p.reshape(1, bp, tp + N_META, *heads), v_p.reshape(1, bp, tp + N_META, *heads),
            nc_p[None, :, lead:],
            k_s.reshape(1, bs, ts, *heads), v_s.reshape(1, bs, ts, *heads),
            nc_s[None, :, lead:])
```

```python
import functools

import jax
import jax.numpy as jnp
from jax import lax
from jax.experimental import pallas as pl
from jax.experimental.pallas import tpu as pltpu

D_MODEL = 1024
N_META = 16
CONV_CH = 512
CONV_WIDTH = 31
CONV_HIST = CONV_WIDTH - 1
SB_HEADS = 8
SB_HEAD_DIM = 64
SB_WIDTH = SB_HEADS * SB_HEAD_DIM
N_GROUPS = 4
EXPERTS_PER_GROUP = 8
N_EXPERTS = N_GROUPS * EXPERTS_PER_GROUP
D_EXPERT = 256
EPS = 1e-6

LANES = 128
HIST_PAD = 32
KEY_BLOCK = 128
KEY_CHUNK = 256
LOG2_E = 1.4426950408889634
MASKED_Z = -1e30
EXIT_LOG2 = 128.0
ROUTER_ROWS = 40
GROUP_ROW = N_EXPERTS
GROUP_ALIGN = 16
PERM_ROWS = 256
EXPERT_STEP = 4
VMEM_LIMIT = 52 * 1024 * 1024
MOE_VMEM_LIMIT = 58 * 1024 * 1024

_F32 = jnp.float32
_BF16 = jnp.bfloat16


def _dot(a, b):
    return jnp.dot(a, b, preferred_element_type=_F32)


def _rms(x, g):
    return x * lax.rsqrt(jnp.mean(x * x, axis=-1, keepdims=True) + EPS) * g


def _conv_chunk(win, cwb_ref, rows):
    lead = HIST_PAD - CONV_HIST
    tiles = rows // 8
    acc = jnp.zeros((tiles, 8, CONV_CH), _F32)
    for s in range(8):
        taps = [k for k in range(CONV_WIDTH) if (lead + k) % 8 == s]
        shifted = win if s == 0 else win[s:s + rows + HIST_PAD - 8, :]
        for k in taps:
            off = lead + k - s
            tap = shifted[off:off + rows, :].reshape(tiles, 8, CONV_CH)
            acc = acc + cwb_ref[8 * k:8 * k + 8, :][None] * tap
    return acc.reshape(rows, CONV_CH)


def _front_kernel(*refs, tile, has_meta, chunks):
    if has_meta:
        x_ref, meta_ref = refs[:2]
        refs = refs[2:]
    else:
        x_ref = refs[0]
        refs = refs[1:]
    (g_ref, w_ref, hist_ref, cw_ref, cb_ref, lg_ref, lb_ref,
     a_ref, q_ref, k_ref, v_ref, nc_ref, h_ref, uh_ref, cwb_ref) = refs
    t = pl.program_id(1)
    g = g_ref[...]
    if has_meta:
        main0 = pl.multiple_of(t * tile, 8)
        h_ref[N_META:, :] = _rms(x_ref[0, pl.ds(main0, tile - N_META), :], g).astype(_BF16)
        head0 = pl.multiple_of(jnp.maximum(t * tile - N_META, 0), 8)
        xh = jnp.where(t == 0, meta_ref[...], x_ref[0, pl.ds(head0, N_META), :])
        h_ref[0:N_META, :] = _rms(xh, g).astype(_BF16)
    else:
        h_ref[...] = _rms(x_ref[0], g).astype(_BF16)

    @pl.when(t == 0)
    def _():
        uh_ref[0:HIST_PAD, :] = hist_ref[0]

    for k in range(CONV_WIDTH):
        cwb_ref[8 * k:8 * k + 8, :] = jnp.broadcast_to(cw_ref[k:k + 1, :], (8, CONV_CH))

    h = h_ref[...]
    c, s = CONV_CH, SB_WIDTH
    uh_ref[HIST_PAD:, :] = _dot(h, w_ref[:, 0:c]) * jax.nn.sigmoid(_dot(h, w_ref[:, c:2 * c]))
    q_ref[0] = (_dot(h, w_ref[:, 2 * c:2 * c + s]) * (SB_HEAD_DIM ** -0.5 * LOG2_E)).astype(_BF16)
    k_ref[0] = _dot(h, w_ref[:, 2 * c + s:2 * c + 2 * s])
    v_ref[0] = _dot(h, w_ref[:, 2 * c + 2 * s:2 * c + 3 * s])

    cb, lg, lb = cb_ref[...], lg_ref[...], lb_ref[...]
    for r0, rows in chunks:
        acc = _conv_chunk(uh_ref[r0:r0 + rows + HIST_PAD, :], cwb_ref, rows) + cb
        mu = jnp.mean(acc, axis=-1, keepdims=True)
        dlt = acc - mu
        var = jnp.mean(dlt * dlt, axis=-1, keepdims=True)
        y = dlt * lax.rsqrt(var + EPS) * lg + lb
        a_ref[0, r0:r0 + rows, :] = (y * jax.nn.sigmoid(y)).astype(_BF16)

    last = uh_ref[tile:tile + HIST_PAD, :]
    nc_ref[0] = last
    uh_ref[0:HIST_PAD, :] = last


def _front(x, meta, g_mix, w_in, hist_padded, conv_w, conv_b, ln_g, ln_b, *, n_tiles, chunk_rows):
    b, t_x, d = x.shape
    t_all = t_x + (N_META if meta is not None else 0)
    tile = t_all // n_tiles
    assert tile * n_tiles == t_all and tile % 16 == 0 and tile >= HIST_PAD
    chunks = tuple((r0, min(chunk_rows, tile - r0)) for r0 in range(0, tile, chunk_rows))
    assert all(rows % 16 == 0 for _, rows in chunks)
    in_w = w_in.shape[1]
    const = lambda shape: pl.BlockSpec(shape, lambda i, j: (0,) * len(shape))
    in_specs = [pl.BlockSpec((1, t_x, d), lambda i, j: (i, 0, 0))]
    args = [x]
    if meta is not None:
        in_specs.append(const((N_META, d)))
        args.append(meta)
    in_specs += [const((1, d)), const((d, in_w)), pl.BlockSpec((1, HIST_PAD, CONV_CH), lambda i, j: (i, 0, 0)),
                 const((CONV_WIDTH, CONV_CH)), const((1, CONV_CH)), const((1, CONV_CH)), const((1, CONV_CH))]
    args += [g_mix, w_in, hist_padded, conv_w, conv_b, ln_g, ln_b]
    row_spec = pl.BlockSpec((1, tile, CONV_CH), lambda i, j: (i, j, 0))
    return pl.pallas_call(
        functools.partial(_front_kernel, tile=tile, has_meta=meta is not None, chunks=chunks),
        grid=(b, n_tiles),
        in_specs=in_specs,
        out_specs=[row_spec] * 4 + [pl.BlockSpec((1, HIST_PAD, CONV_CH), lambda i, j: (i, 0, 0))],
        out_shape=[jax.ShapeDtypeStruct((b, t_all, CONV_CH), _BF16),
                   jax.ShapeDtypeStruct((b, t_all, SB_WIDTH), _BF16),
                   jax.ShapeDtypeStruct((b, t_all, SB_WIDTH), _F32),
                   jax.ShapeDtypeStruct((b, t_all, SB_WIDTH), _F32),
                   jax.ShapeDtypeStruct((b, HIST_PAD, CONV_CH), _F32)],
        scratch_shapes=[pltpu.VMEM((tile, d), _BF16),
                        pltpu.VMEM((HIST_PAD + tile, CONV_CH), _F32),
                        pltpu.VMEM((8 * CONV_WIDTH, CONV_CH), _F32)],
        compiler_params=pltpu.CompilerParams(
            dimension_semantics=("arbitrary", "arbitrary"), vmem_limit_bytes=VMEM_LIMIT),
        name=f"front_t{t_all}",
    )(*args)


def _sb_kernel(*refs, mq, g_lock, n_super, q_off, front_pad, n_past):
    if n_past:
        q_ref, kn_ref, vn_ref, kp_ref, vp_ref, g_ref, tri_ref, o_ref = refs[:8]
    else:
        q_ref, kn_ref, vn_ref, g_ref, tri_ref, o_ref = refs[:6]
    kb_ref, v0_ref, v1_ref, acc_ref, carry_ref = refs[-5:]
    t_new = kn_ref.shape[1]
    rows = kb_ref.shape[0]
    lane_row = lax.broadcasted_iota(jnp.int32, (1, LANES), 1) < SB_HEAD_DIM

    def fill(r0, n, k, v):
        kb_ref[r0:r0 + n, :] = k.astype(_BF16)
        v0_ref[r0:r0 + n, :] = jnp.where(lane_row, v, 0.0).astype(_BF16)
        v1_ref[r0:r0 + n, :] = jnp.where(lane_row, 0.0, v).astype(_BF16)

    new0 = front_pad + n_past
    if front_pad:
        fill(0, front_pad, jnp.zeros((front_pad, LANES), _F32), jnp.zeros((front_pad, LANES), _F32))
    if n_past:
        fill(front_pad, n_past, kp_ref[0], vp_ref[0])
    fill(new0, t_new, kn_ref[0], vn_ref[0])
    tail = rows - new0 - t_new
    if tail:
        fill(new0 + t_new, tail, jnp.zeros((tail, LANES), _F32), jnp.zeros((tail, LANES), _F32))
    if q_off:
        o_ref[0, 0:q_off, :] = jnp.zeros((q_off, LANES), _BF16)

    tri = tri_ref[...]
    gsb = g_ref[...]
    lane_lo = lax.broadcasted_iota(jnp.int32, (mq, LANES), 1) < SB_HEAD_DIM
    col_minus_row = (lax.broadcasted_iota(jnp.int32, (2 * mq, KEY_BLOCK), 1) + KEY_BLOCK
                     - lax.broadcasted_iota(jnp.int32, (2 * mq, KEY_BLOCK), 0) % mq)
    assert (new0 + q_off) % KEY_BLOCK == 0 and front_pad >= KEY_BLOCK
    assert mq == KEY_BLOCK or g_lock * n_super == 1
    end0 = new0 + q_off + KEY_BLOCK
    nt_dims = (((1,), (1,)), ((), ()))

    def superblock(sb, unused):
        i0 = sb * g_lock
        q0s = [pl.multiple_of(q_off + (i0 + g) * mq, 16) for g in range(g_lock)]
        qhs = []
        for g in range(g_lock):
            qb = q_ref[0, pl.ds(q0s[g], mq), :]
            zero_q = jnp.zeros_like(qb)
            qhs.append(jnp.concatenate([jnp.where(lane_lo, qb, zero_q), jnp.where(lane_lo, zero_q, qb)],
                                       axis=0))
        acc_ref[...] = jnp.zeros_like(acc_ref)
        carry_ref[...] = jnp.zeros_like(carry_ref)

        def sweep(state):
            n = state[0]
            low = jnp.full((1, LANES), jnp.inf, _F32)
            starts = [end0 + (i0 + g) * mq - (n + 1) * KEY_CHUNK for g in range(g_lock)]
            r0s = [pl.multiple_of(jnp.maximum(st, 0), KEY_BLOCK) for st in starts]
            zs, cts = [], []
            for g in range(g_lock):
                z = lax.dot_general(qhs[g], kb_ref[pl.ds(r0s[g], KEY_CHUNK), :], nt_dims,
                                    preferred_element_type=_F32)
                hidden_r = jnp.where(col_minus_row < KEY_BLOCK + n * KEY_CHUNK, z[:, KEY_BLOCK:], MASKED_Z)
                zs.append(jnp.concatenate([z[:, :KEY_BLOCK], hidden_r], axis=1))
            for g in range(g_lock):
                z = zs[g]
                s = jnp.maximum(z, 0.0) + jnp.log2(1.0 + jnp.exp2(-jnp.abs(z)))
                hi = s.astype(_BF16)
                lo = (s - hi.astype(_F32)).astype(_BF16)
                cts.append(_dot(jnp.concatenate(
                    [jnp.concatenate([hi[:, KEY_BLOCK:], lo[:, KEY_BLOCK:]], axis=1),
                     jnp.concatenate([hi[:, :KEY_BLOCK], lo[:, :KEY_BLOCK]], axis=1)], axis=0), tri))
            for g in range(g_lock):
                ct_r, ct_l = cts[g][:2 * mq], cts[g][2 * mq:]
                carry = carry_ref[g]
                c_r = ct_r[:, :KEY_BLOCK] + carry
                carry = carry + ct_r[:, KEY_BLOCK:]
                c_l = ct_l[:, :KEY_BLOCK] + carry
                carry = carry + ct_l[:, KEY_BLOCK:]
                carry_ref[g] = carry
                w = jnp.exp2(zs[g] - jnp.concatenate([c_l, c_r], axis=1)).astype(_BF16)
                pv = (_dot(w[:mq], v0_ref[pl.ds(r0s[g], KEY_CHUNK), :])
                      + _dot(w[mq:], v1_ref[pl.ds(r0s[g], KEY_CHUNK), :]))
                acc_ref[g] += jnp.where(starts[g] >= 0, pv, 0.0)
                done = jnp.where(starts[g] - KEY_CHUNK >= 0, 0.0, 2 * EXIT_LOG2)
                low = jnp.minimum(low, jnp.min(carry, axis=0, keepdims=True) + done)
            return n + 1, jnp.where(jnp.min(low) > EXIT_LOG2, 0, 1)

        lax.while_loop(lambda st: st[1] > 0, sweep, (jnp.int32(0), jnp.int32(1)))

        for g in range(g_lock):
            o = acc_ref[g]
            o2 = o * o
            s_all = jnp.sum(o2, axis=-1, keepdims=True)
            s_lo = jnp.sum(jnp.where(lane_lo, o2, 0.0), axis=-1, keepdims=True)
            inv = jnp.where(lane_lo, lax.rsqrt(s_lo / SB_HEAD_DIM + EPS),
                            lax.rsqrt((s_all - s_lo) / SB_HEAD_DIM + EPS))
            o_ref[0, pl.ds(q0s[g], mq), :] = (o * inv * gsb).astype(_BF16)
        return unused

    lax.fori_loop(0, n_super, superblock, 0)


def _tri_matrix():
    r = lax.broadcasted_iota(jnp.int32, (2 * KEY_BLOCK, 2 * KEY_BLOCK), 0) % KEY_BLOCK
    c = lax.broadcasted_iota(jnp.int32, (2 * KEY_BLOCK, 2 * KEY_BLOCK), 1)
    return jnp.where((c >= KEY_BLOCK) | (r >= c), 1.0, 0.0).astype(_BF16)


def _stick_breaking(q, k_new, v_new, k_past, v_past, g_sb, *, mq, g_lock, q_off, front_pad):
    b, t_new, _ = k_new.shape
    n_past = 0 if k_past is None else k_past.shape[1]
    n_super = (t_new - q_off) // (mq * g_lock)
    assert n_super * mq * g_lock == t_new - q_off
    last_end = front_pad + n_past + q_off + KEY_BLOCK + (n_super * g_lock - 1) * mq
    rows = max(pl.cdiv(front_pad + n_past + t_new, KEY_BLOCK) * KEY_BLOCK, last_end)
    pairs = SB_WIDTH // LANES
    new_spec = pl.BlockSpec((1, t_new, LANES), lambda i, p: (i, 0, p))
    in_specs = [new_spec, new_spec, new_spec]
    args = [q, k_new, v_new]
    if n_past:
        past_spec = pl.BlockSpec((1, n_past, LANES), lambda i, p: (i, 0, p))
        in_specs += [past_spec, past_spec]
        args += [k_past, v_past]
    in_specs += [pl.BlockSpec((1, LANES), lambda i, p: (0, p)),
                 pl.BlockSpec((2 * KEY_BLOCK, 2 * KEY_BLOCK), lambda i, p: (0, 0))]
    args += [g_sb, _tri_matrix()]
    return pl.pallas_call(
        functools.partial(_sb_kernel, mq=mq, g_lock=g_lock, n_super=n_super, q_off=q_off,
                          front_pad=front_pad, n_past=n_past),
        grid=(b, pairs),
        in_specs=in_specs,
        out_specs=pl.BlockSpec((1, t_new, LANES), lambda i, p: (i, 0, p)),
        out_shape=jax.ShapeDtypeStruct((b, t_new, SB_WIDTH), _BF16),
        scratch_shapes=[pltpu.VMEM((rows, LANES), _BF16)] * 3
        + [pltpu.VMEM((g_lock, mq, LANES), _F32), pltpu.VMEM((g_lock, 2 * mq, LANES), _F32)],
        compiler_params=pltpu.CompilerParams(
            dimension_semantics=("arbitrary", "arbitrary"), vmem_limit_bytes=VMEM_LIMIT),
        name=f"stick_breaking_t{t_new}",
    )(*args)


def _split_bf16(x):
    hi = x.astype(_BF16)
    return hi, (x - hi.astype(_F32)).astype(_BF16)


def _mix_kernel(a_ref, b_ref, x_ref, wa_ref, wb_ref, g_ref, wrh_ref, wrl_ref, br_ref,
                x2_ref, h2_ref, gates_ref, *, n, off):
    r = pl.program_id(1)
    a0 = pl.multiple_of(off + r * n, 16)
    x2 = (x_ref[...] + _dot(a_ref[0, pl.ds(a0, n), :], wa_ref[...])
          + _dot(b_ref[0, pl.ds(a0, n), :], wb_ref[...]))
    x2_ref[...] = x2
    hf = _rms(x2, g_ref[...])
    h_hi, h_lo = _split_bf16(hf)
    h2_ref[...] = h_hi
    nt = (((1,), (1,)), ((), ()))
    wrh = wrh_ref[...]
    logits = (lax.dot_general(wrh, h_hi, nt, preferred_element_type=_F32)
              + lax.dot_general(wrh, h_lo, nt, preferred_element_type=_F32)
              + lax.dot_general(wrl_ref[...], h_hi, nt, preferred_element_type=_F32))
    logits = logits[:ROUTER_ROWS] + jnp.tile(br_ref[...], (1, n // LANES))[:ROUTER_ROWS]
    row = lax.broadcasted_iota(jnp.int32, (ROUTER_ROWS, n), 0)
    neg = -jnp.inf
    big = 4 * LANES
    is_g = (row >= N_EXPERTS) & (row < N_EXPERTS + N_GROUPS)
    lg = jnp.where(is_g, logits, neg)
    g_max = jnp.max(lg, axis=0, keepdims=True)
    p_g_sel = 1.0 / jnp.sum(jnp.where(is_g, jnp.exp(lg - g_max), 0.0), axis=0, keepdims=True)
    g_sel = jnp.min(jnp.where(lg == g_max, row, big), axis=0, keepdims=True) - N_EXPERTS
    in_grp = (row < N_EXPERTS) & ((row // EXPERTS_PER_GROUP) == g_sel)
    le = jnp.where(in_grp, logits, neg)
    e_max = jnp.max(le, axis=0, keepdims=True)
    pe = jnp.where(in_grp, jnp.exp(le - e_max), 0.0)
    pe = pe / jnp.sum(pe, axis=0, keepdims=True)
    pe = jnp.where(in_grp, pe, -1.0)
    top1 = jnp.max(pe, axis=0, keepdims=True)
    i1 = jnp.min(jnp.where(pe == top1, row, big), axis=0, keepdims=True)
    pe_rest = jnp.where(row == i1, -1.0, pe)
    top2 = jnp.max(pe_rest, axis=0, keepdims=True)
    i2 = jnp.min(jnp.where(pe_rest == top2, row, big), axis=0, keepdims=True)
    denom = top1 + top2
    gates = jnp.where(row == i1, top1 / denom, jnp.where(row == i2, top2 / denom, 0.0)) * p_g_sel
    gates = jnp.where(row == GROUP_ROW, g_sel.astype(_F32), gates)
    gates_ref[...] = jnp.concatenate([gates, jnp.zeros((LANES - ROUTER_ROWS, n), _F32)], axis=0)


def _mix(a, b_out, x, wa, wb, g_ffn, wr_hi, wr_lo, br, *, n, off):
    bsz, t_a, _ = a.shape
    rows, d = x.shape
    t_x = rows // bsz
    assert t_x % n == 0 and n % LANES == 0
    steps = t_x // n
    res = pl.BlockSpec((1, t_a, CONV_CH), lambda i, r: (i, 0, 0))
    tile = pl.BlockSpec((n, d), lambda i, r: (i * steps + r, 0))
    full = lambda shape: pl.BlockSpec(shape, lambda i, r: (0,) * len(shape))
    return pl.pallas_call(
        functools.partial(_mix_kernel, n=n, off=off),
        grid=(bsz, steps),
        in_specs=[res, res, tile, full(wa.shape), full(wb.shape), full((1, d)),
                  full(wr_hi.shape), full(wr_lo.shape), full((LANES, LANES))],
        out_specs=[tile, tile, pl.BlockSpec((LANES, n), lambda i, r: (0, i * steps + r))],
        out_shape=[jax.ShapeDtypeStruct((rows, d), _F32),
                   jax.ShapeDtypeStruct((rows, d), _BF16),
                   jax.ShapeDtypeStruct((LANES, rows), _F32)],
        compiler_params=pltpu.CompilerParams(
            dimension_semantics=("arbitrary", "arbitrary"), vmem_limit_bytes=VMEM_LIMIT),
        name=f"mix_r{rows}",
    )(a, b_out, x, wa, wb, g_ffn, wr_hi, wr_lo, br)


def _moe_kernel(h_ref, gt_ref, x2_ref, win_ref, wout_ref, gf_ref, y_ref,
                hs_ref, gs_ref, ys_ref, ut_ref, pos_ref, meta_ref, *, tm, tms_p, ffn_rows):
    i = pl.program_id(0)
    s = pl.program_id(1)
    steps_per_group = EXPERTS_PER_GROUP // EXPERT_STEP
    nt = (((1,), (1,)), ((), ()))
    tms = hs_ref.shape[0]

    @pl.when((i == 0) & (s == 0))
    def _():
        ut_ref[...] = jnp.where(lax.broadcasted_iota(jnp.int32, (tm, tm), 0)
                                < lax.broadcasted_iota(jnp.int32, (tm, tm), 1), 1.0, 0.0).astype(_BF16)

    @pl.when(s == 0)
    def _():
        gt = gt_ref[...]
        gid = gt[GROUP_ROW:GROUP_ROW + 1, :].astype(jnp.int32)
        grow = lax.broadcasted_iota(jnp.int32, (8, tm), 0)
        onehot = jnp.where(grow == gid, 1.0, 0.0)
        rank = _dot(onehot.astype(_BF16), ut_ref[...])
        off = jnp.int32(0)
        off_rows = jnp.zeros((8, tm), _F32)
        for g in range(N_GROUPS):
            cnt = jnp.sum(jnp.where(grow == g, onehot, 0.0)).astype(jnp.int32)
            meta_ref[g] = off
            meta_ref[N_GROUPS + g] = (cnt + ffn_rows - 1) // ffn_rows
            off_rows = jnp.where(grow == g, off.astype(_F32), off_rows)
            off = off + (cnt + GROUP_ALIGN - 1) // GROUP_ALIGN * GROUP_ALIGN
        pos = jnp.sum(onehot * (rank + off_rows), axis=0, keepdims=True)
        pos_ref[...] = jnp.transpose(jnp.broadcast_to(pos, (LANES, tm)))
        pos_i = pos.astype(jnp.int32)
        g_hi, g_lo = _split_bf16(gt)
        g_hl = jnp.concatenate([g_hi, g_lo], axis=0)
        h = h_ref[...]
        for c in range(tms_p // PERM_ROWS):
            prow = lax.broadcasted_iota(jnp.int32, (PERM_ROWS, tm), 0) + c * PERM_ROWS
            perm = jnp.where(prow == pos_i, 1.0, 0.0).astype(_BF16)
            rows = pl.ds(c * PERM_ROWS, PERM_ROWS)
            hs_ref[rows, :] = _dot(perm, h).astype(_BF16)
            g2 = lax.dot_general(perm, g_hl, nt, preferred_element_type=_F32)
            gs_ref[rows, :] = g2[:, :LANES] + g2[:, LANES:]
        hs_ref[tms_p:, :] = jnp.zeros((tms - tms_p, D_MODEL), _BF16)
        gs_ref[tms_p:, :] = jnp.zeros((tms - tms_p, LANES), _F32)
        ys_ref[...] = jnp.zeros_like(ys_ref)

    grp = s // steps_per_group
    off = meta_ref[grp]
    lane = lax.broadcasted_iota(jnp.int32, (ffn_rows, LANES), 1)

    def ffn(b, carry):
        rows = pl.ds(pl.multiple_of(off + b * ffn_rows, GROUP_ALIGN), ffn_rows)
        hb = hs_ref[rows, :]
        gsb = gs_ref[rows, :]
        acc = jnp.zeros((ffn_rows, D_MODEL), _F32)
        for j in range(EXPERT_STEP):
            gu = _dot(hb, win_ref[j])
            act = (jax.nn.silu(gu[:, :D_EXPERT]) * gu[:, D_EXPERT:]).astype(_BF16)
            gate = jnp.sum(jnp.where(lane == s * EXPERT_STEP + j, gsb, 0.0), axis=-1, keepdims=True)
            acc = acc + gate * _dot(act, wout_ref[j])
        ys_ref[rows, :] += acc
        return carry

    lax.fori_loop(0, meta_ref[N_GROUPS + grp], ffn, 0)

    @pl.when(s == pl.num_programs(1) - 1)
    def _():
        hs_ref[0:tms_p, :] = ys_ref[0:tms_p, :].astype(_BF16)
        for c in range(tm // PERM_ROWS):
            rows = pl.ds(c * PERM_ROWS, PERM_ROWS)
            pos_c = jnp.tile(pos_ref[rows, :], (1, tms_p // LANES)).astype(jnp.int32)
            unperm = jnp.where(lax.broadcasted_iota(jnp.int32, (PERM_ROWS, tms_p), 1) == pos_c,
                               1.0, 0.0).astype(_BF16)
            y_ref[rows, :] = _rms(x2_ref[rows, :] + _dot(unperm, hs_ref[0:tms_p, :]), gf_ref[...])


def _moe(h2, gates_t, x2, w_e_in, w_e_out, g_final, *, tm):
    rows, d = h2.shape
    assert rows % tm == 0 and tm % PERM_ROWS == 0
    tms_p = pl.cdiv(tm + N_GROUPS * (GROUP_ALIGN - 1), PERM_ROWS) * PERM_ROWS
    ffn_rows = pl.cdiv(tm * 9 // (8 * N_GROUPS), GROUP_ALIGN) * GROUP_ALIGN
    tms = tms_p + ffn_rows
    n_steps = N_EXPERTS // EXPERT_STEP
    tok = lambda w: pl.BlockSpec((tm, w), lambda i, s: (i, 0))
    return pl.pallas_call(
        functools.partial(_moe_kernel, tm=tm, tms_p=tms_p, ffn_rows=ffn_rows),
        grid=(rows // tm, n_steps),
        in_specs=[tok(d), pl.BlockSpec((LANES, tm), lambda i, s: (0, i)), tok(d),
                  pl.BlockSpec((EXPERT_STEP, d, 2 * D_EXPERT), lambda i, s: (s, 0, 0)),
                  pl.BlockSpec((EXPERT_STEP, D_EXPERT, d), lambda i, s: (s, 0, 0)),
                  pl.BlockSpec((1, d), lambda i, s: (0, 0))],
        out_specs=tok(d),
        out_shape=jax.ShapeDtypeStruct((rows, d), _F32),
        scratch_shapes=[pltpu.VMEM((tms, d), _BF16),
                        pltpu.VMEM((tms, LANES), _F32),
                        pltpu.VMEM((tms, d), _F32),
                        pltpu.VMEM((tm, tm), _BF16),
                        pltpu.VMEM((tm, LANES), _F32),
                        pltpu.SMEM((2 * N_GROUPS,), jnp.int32)],
        compiler_params=pltpu.CompilerParams(
            dimension_semantics=("arbitrary", "arbitrary"), vmem_limit_bytes=MOE_VMEM_LIMIT),
        name=f"moe_r{rows}",
    )(h2, gates_t, x2, w_e_in, w_e_out, g_final)


def kernel(x_prompt, x_sample, cache_k, cache_v, cache_conv, meta_tokens, g_mix, w_in, conv_w, conv_b,
           conv_ln_g, conv_ln_b, g_sb, w_out, g_ffn, w_router_group, b_router_group, w_router_expert,
           b_router_expert, w_expert_in, w_expert_out, g_final):
    depth = w_in.shape[0]
    assert depth == 1, "single-layer trunk"
    bp, tp, d = x_prompt.shape
    bs, ts, _ = x_sample.shape
    n_past = cache_k.shape[2]

    w_in_b = w_in[0].astype(_BF16)
    wa = w_out[0, :CONV_CH].astype(_BF16)
    wb = w_out[0, CONV_CH:].astype(_BF16)
    w_e_in = w_expert_in[0].astype(_BF16)
    w_e_out = w_expert_out[0].astype(_BF16)
    pad = LANES - N_EXPERTS - N_GROUPS
    wr = jnp.concatenate([w_router_expert[0].T, w_router_group[0].T, jnp.zeros((pad, d), _F32)], axis=0)
    wr_hi = wr.astype(_BF16)
    wr_lo = (wr - wr_hi.astype(_F32)).astype(_BF16)
    br = jnp.concatenate([b_router_expert[0], b_router_group[0], jnp.zeros((pad,), _F32)])
    br = jnp.broadcast_to(br[:, None], (LANES, LANES))
    g_mix2, g_ffn2, g_fin2, g_sb2 = g_mix[0][None], g_ffn[0][None], g_final[None], g_sb[0][None]
    cb, lg, lb = conv_b[0][None], conv_ln_g[0][None], conv_ln_b[0][None]

    def tail(a, b_out, x, *, n, off, tm):
        rows = x.shape[0] * x.shape[1]
        x2, h2, gates_t = _mix(a, b_out, x.reshape(rows, d), wa, wb, g_ffn2, wr_hi, wr_lo, br, n=n, off=off)
        return _moe(h2, gates_t, x2, w_e_in, w_e_out, g_fin2, tm=tm).reshape(x.shape)

    a_p, q_p, k_p, v_p, nc_p = _front(x_prompt, meta_tokens, g_mix2, w_in_b,
                                      jnp.zeros((bp, HIST_PAD, CONV_CH), _F32), conv_w[0], cb, lg, lb,
                                      n_tiles=3, chunk_rows=256)
    b_p = _stick_breaking(q_p, k_p, v_p, None, None, g_sb2, mq=KEY_BLOCK, g_lock=8, q_off=N_META,
                          front_pad=2 * KEY_BLOCK - N_META)
    y_p = tail(a_p, b_p, x_prompt, n=512, off=N_META, tm=1024)

    hist = jnp.pad(cache_conv[0], ((0, 0), (HIST_PAD - CONV_HIST, 0), (0, 0)))
    a_s, q_s, k_s, v_s, nc_s = _front(x_sample, None, g_mix2, w_in_b, hist, conv_w[0], cb, lg, lb,
                                      n_tiles=1, chunk_rows=ts)
    b_s = _stick_breaking(q_s, k_s, v_s, cache_k[0].reshape(bs, n_past, SB_WIDTH),
                          cache_v[0].reshape(bs, n_past, SB_WIDTH), g_sb2, mq=ts, g_lock=1, q_off=0,
                          front_pad=KEY_BLOCK)
    flat = lambda t: t.reshape(1, bs * ts, t.shape[-1])
    y_s = tail(flat(a_s), flat(b_s), flat(x_sample), n=bs * ts, off=0, tm=bs * ts).reshape(x_sample.shape)

    heads = (SB_HEADS, SB_HEAD_DIM)
    lead = HIST_PAD - CONV_HIST
    return (y_p, y_s,
            k_p.reshape(1, bp, tp + N_META, *heads), v_p.reshape(1, bp, tp + N_META, *heads),
            nc_p[None, :, lead:],
            k_s.reshape(1, bs, ts, *heads), v_s.reshape(1, bs, ts, *heads),
            nc_s[None, :, lead:])
```

```python
import functools

import jax
import jax.numpy as jnp
from jax import lax
from jax.experimental import pallas as pl
from jax.experimental.pallas import tpu as pltpu

D_MODEL = 1024
N_META = 16
CONV_CH = 512
CONV_WIDTH = 31
CONV_HIST = CONV_WIDTH - 1
SB_HEADS = 8
SB_HEAD_DIM = 64
SB_WIDTH = SB_HEADS * SB_HEAD_DIM
N_GROUPS = 4
EXPERTS_PER_GROUP = 8
N_EXPERTS = N_GROUPS * EXPERTS_PER_GROUP
D_EXPERT = 256
EPS = 1e-6

LANES = 128
HIST_PAD = 32
KEY_BLOCK = 128
KEY_CHUNK = 256
LOG2_E = 1.4426950408889634
MASKED_Z = -1e30
EXIT_LOG2 = 128.0
ROUTER_ROWS = 40
GROUP_ROW = N_EXPERTS
GROUP_ALIGN = 16
PERM_ROWS = 256
EXPERT_STEP = 4
VMEM_LIMIT = 52 * 1024 * 1024
MOE_VMEM_LIMIT = 58 * 1024 * 1024

_F32 = jnp.float32
_BF16 = jnp.bfloat16


def _dot(a, b):
    return jnp.dot(a, b, preferred_element_type=_F32)


def _rms(x, g):
    return x * lax.rsqrt(jnp.mean(x * x, axis=-1, keepdims=True) + EPS) * g


def _conv_chunk(win, cwb_ref, rows):
    lead = HIST_PAD - CONV_HIST
    tiles = rows // 8
    acc = jnp.zeros((tiles, 8, CONV_CH), _F32)
    for s in range(8):
        taps = [k for k in range(CONV_WIDTH) if (lead + k) % 8 == s]
        shifted = win if s == 0 else win[s:s + rows + HIST_PAD - 8, :]
        for k in taps:
            off = lead + k - s
            tap = shifted[off:off + rows, :].reshape(tiles, 8, CONV_CH)
            acc = acc + cwb_ref[8 * k:8 * k + 8, :][None] * tap
    return acc.reshape(rows, CONV_CH)


def _front_kernel(*refs, tile, has_meta, chunks):
    if has_meta:
        x_ref, meta_ref = refs[:2]
        refs = refs[2:]
    else:
        x_ref = refs[0]
        refs = refs[1:]
    (g_ref, w_ref, hist_ref, cw_ref, cb_ref, lg_ref, lb_ref,
     a_ref, q_ref, k_ref, v_ref, nc_ref, h_ref, uh_ref, cwb_ref) = refs
    t = pl.program_id(1)
    g = g_ref[...]
    if has_meta:
        main0 = pl.multiple_of(t * tile, 8)
        h_ref[N_META:, :] = _rms(x_ref[0, pl.ds(main0, tile - N_META), :], g).astype(_BF16)
        head0 = pl.multiple_of(jnp.maximum(t * tile - N_META, 0), 8)
        xh = jnp.where(t == 0, meta_ref[...], x_ref[0, pl.ds(head0, N_META), :])
        h_ref[0:N_META, :] = _rms(xh, g).astype(_BF16)
    else:
        h_ref[...] = _rms(x_ref[0], g).astype(_BF16)

    @pl.when(t == 0)
    def _():
        uh_ref[0:HIST_PAD, :] = hist_ref[0]

    for k in range(CONV_WIDTH):
        cwb_ref[8 * k:8 * k + 8, :] = jnp.broadcast_to(cw_ref[k:k + 1, :], (8, CONV_CH))

    h = h_ref[...]
    c, s = CONV_CH, SB_WIDTH
    uh_ref[HIST_PAD:, :] = _dot(h, w_ref[:, 0:c]) * jax.nn.sigmoid(_dot(h, w_ref[:, c:2 * c]))
    q_ref[0] = (_dot(h, w_ref[:, 2 * c:2 * c + s]) * (SB_HEAD_DIM ** -0.5 * LOG2_E)).astype(_BF16)
    k_ref[0] = _dot(h, w_ref[:, 2 * c + s:2 * c + 2 * s])
    v_ref[0] = _dot(h, w_ref[:, 2 * c + 2 * s:2 * c + 3 * s])

    cb, lg, lb = cb_ref[...], lg_ref[...], lb_ref[...]
    for r0, rows in chunks:
        acc = _conv_chunk(uh_ref[r0:r0 + rows + HIST_PAD, :], cwb_ref, rows) + cb
        mu = jnp.mean(acc, axis=-1, keepdims=True)
        dlt = acc - mu
        var = jnp.mean(dlt * dlt, axis=-1, keepdims=True)
        y = dlt * lax.rsqrt(var + EPS) * lg + lb
        a_ref[0, r0:r0 + rows, :] = (y * jax.nn.sigmoid(y)).astype(_BF16)

    last = uh_ref[tile:tile + HIST_PAD, :]
    nc_ref[0] = last
    uh_ref[0:HIST_PAD, :] = last


def _front(x, meta, g_mix, w_in, hist_padded, conv_w, conv_b, ln_g, ln_b, *, n_tiles, chunk_rows):
    b, t_x, d = x.shape
    t_all = t_x + (N_META if meta is not None else 0)
    tile = t_all // n_tiles
    assert tile * n_tiles == t_all and tile % 16 == 0 and tile >= HIST_PAD
    chunks = tuple((r0, min(chunk_rows, tile - r0)) for r0 in range(0, tile, chunk_rows))
    assert all(rows % 16 == 0 for _, rows in chunks)
    in_w = w_in.shape[1]
    const = lambda shape: pl.BlockSpec(shape, lambda i, j: (0,) * len(shape))
    in_specs = [pl.BlockSpec((1, t_x, d), lambda i, j: (i, 0, 0))]
    args = [x]
    if meta is not None:
        in_specs.append(const((N_META, d)))
        args.append(meta)
    in_specs += [const((1, d)), const((d, in_w)), pl.BlockSpec((1, HIST_PAD, CONV_CH), lambda i, j: (i, 0, 0)),
                 const((CONV_WIDTH, CONV_CH)), const((1, CONV_CH)), const((1, CONV_CH)), const((1, CONV_CH))]
    args += [g_mix, w_in, hist_padded, conv_w, conv_b, ln_g, ln_b]
    row_spec = pl.BlockSpec((1, tile, CONV_CH), lambda i, j: (i, j, 0))
    return pl.pallas_call(
        functools.partial(_front_kernel, tile=tile, has_meta=meta is not None, chunks=chunks),
        grid=(b, n_tiles),
        in_specs=in_specs,
        out_specs=[row_spec] * 4 + [pl.BlockSpec((1, HIST_PAD, CONV_CH), lambda i, j: (i, 0, 0))],
        out_shape=[jax.ShapeDtypeStruct((b, t_all, CONV_CH), _BF16),
                   jax.ShapeDtypeStruct((b, t_all, SB_WIDTH), _BF16),
                   jax.ShapeDtypeStruct((b, t_all, SB_WIDTH), _F32),
                   jax.ShapeDtypeStruct((b, t_all, SB_WIDTH), _F32),
                   jax.ShapeDtypeStruct((b, HIST_PAD, CONV_CH), _F32)],
        scratch_shapes=[pltpu.VMEM((tile, d), _BF16),
                        pltpu.VMEM((HIST_PAD + tile, CONV_CH), _F32),
                        pltpu.VMEM((8 * CONV_WIDTH, CONV_CH), _F32)],
        compiler_params=pltpu.CompilerParams(
            dimension_semantics=("arbitrary", "arbitrary"), vmem_limit_bytes=VMEM_LIMIT),
        name=f"front_t{t_all}",
    )(*args)


def _sb_kernel(*refs, mq, g_lock, n_super, q_off, front_pad, n_past):
    if n_past:
        q_ref, kn_ref, vn_ref, kp_ref, vp_ref, g_ref, tri_ref, o_ref = refs[:8]
    else:
        q_ref, kn_ref, vn_ref, g_ref, tri_ref, o_ref = refs[:6]
    kb_ref, v0_ref, v1_ref, acc_ref, carry_ref = refs[-5:]
    t_new = kn_ref.shape[1]
    rows = kb_ref.shape[0]
    lane_row = lax.broadcasted_iota(jnp.int32, (1, LANES), 1) < SB_HEAD_DIM

    def fill(r0, n, k, v):
        kb_ref[r0:r0 + n, :] = k.astype(_BF16)
        v0_ref[r0:r0 + n, :] = jnp.where(lane_row, v, 0.0).astype(_BF16)
        v1_ref[r0:r0 + n, :] = jnp.where(lane_row, 0.0, v).astype(_BF16)

    new0 = front_pad + n_past
    if front_pad:
        fill(0, front_pad, jnp.zeros((front_pad, LANES), _F32), jnp.zeros((front_pad, LANES), _F32))
    if n_past:
        fill(front_pad, n_past, kp_ref[0], vp_ref[0])
    fill(new0, t_new, kn_ref[0], vn_ref[0])
    tail = rows - new0 - t_new
    if tail:
        fill(new0 + t_new, tail, jnp.zeros((tail, LANES), _F32), jnp.zeros((tail, LANES), _F32))
    if q_off:
        o_ref[0, 0:q_off, :] = jnp.zeros((q_off, LANES), _BF16)

    tri = tri_ref[...]
    gsb = g_ref[...]
    lane_lo = lax.broadcasted_iota(jnp.int32, (mq, LANES), 1) < SB_HEAD_DIM
    col_minus_row = (lax.broadcasted_iota(jnp.int32, (2 * mq, KEY_BLOCK), 1) + KEY_BLOCK
                     - lax.broadcasted_iota(jnp.int32, (2 * mq, KEY_BLOCK), 0) % mq)
    assert (new0 + q_off) % KEY_BLOCK == 0 and front_pad >= KEY_BLOCK
    assert mq == KEY_BLOCK or g_lock * n_super == 1
    end0 = new0 + q_off + KEY_BLOCK
    nt_dims = (((1,), (1,)), ((), ()))

    def superblock(sb, unused):
        i0 = sb * g_lock
        q0s = [pl.multiple_of(q_off + (i0 + g) * mq, 16) for g in range(g_lock)]
        qhs = []
        for g in range(g_lock):
            qb = q_ref[0, pl.ds(q0s[g], mq), :]
            zero_q = jnp.zeros_like(qb)
            qhs.append(jnp.concatenate([jnp.where(lane_lo, qb, zero_q), jnp.where(lane_lo, zero_q, qb)],
                                       axis=0))
        acc_ref[...] = jnp.zeros_like(acc_ref)
        carry_ref[...] = jnp.zeros_like(carry_ref)

        def sweep(state):
            n = state[0]
            low = jnp.full((1, LANES), jnp.inf, _F32)
            starts = [end0 + (i0 + g) * mq - (n + 1) * KEY_CHUNK for g in range(g_lock)]
            r0s = [pl.multiple_of(jnp.maximum(st, 0), KEY_BLOCK) for st in starts]
            zs, cts = [], []
            for g in range(g_lock):
                z = lax.dot_general(qhs[g], kb_ref[pl.ds(r0s[g], KEY_CHUNK), :], nt_dims,
                                    preferred_element_type=_F32)
                hidden_r = jnp.where(col_minus_row < KEY_BLOCK + n * KEY_CHUNK, z[:, KEY_BLOCK:], MASKED_Z)
                zs.append(jnp.concatenate([z[:, :KEY_BLOCK], hidden_r], axis=1))
            for g in range(g_lock):
                z = zs[g]
                s = jnp.maximum(z, 0.0) + jnp.log2(1.0 + jnp.exp2(-jnp.abs(z)))
                hi = s.astype(_BF16)
                lo = (s - hi.astype(_F32)).astype(_BF16)
                cts.append(_dot(jnp.concatenate(
                    [jnp.concatenate([hi[:, KEY_BLOCK:], lo[:, KEY_BLOCK:]], axis=1),
                     jnp.concatenate([hi[:, :KEY_BLOCK], lo[:, :KEY_BLOCK]], axis=1)], axis=0), tri))
            for g in range(g_lock):
                ct_r, ct_l = cts[g][:2 * mq], cts[g][2 * mq:]
                carry = carry_ref[g]
                c_r = ct_r[:, :KEY_BLOCK] + carry
                carry = carry + ct_r[:, KEY_BLOCK:]
                c_l = ct_l[:, :KEY_BLOCK] + carry
                carry = carry + ct_l[:, KEY_BLOCK:]
                carry_ref[g] = carry
                w = jnp.exp2(zs[g] - jnp.concatenate([c_l, c_r], axis=1)).astype(_BF16)
                pv = (_dot(w[:mq], v0_ref[pl.ds(r0s[g], KEY_CHUNK), :])
                      + _dot(w[mq:], v1_ref[pl.ds(r0s[g], KEY_CHUNK), :]))
                acc_ref[g] += jnp.where(starts[g] >= 0, pv, 0.0)
                done = jnp.where(starts[g] - KEY_CHUNK >= 0, 0.0, 2 * EXIT_LOG2)
                low = jnp.minimum(low, jnp.min(carry, axis=0, keepdims=True) + done)
            return n + 1, jnp.where(jnp.min(low) > EXIT_LOG2, 0, 1)

        lax.while_loop(lambda st: st[1] > 0, sweep, (jnp.int32(0), jnp.int32(1)))

        for g in range(g_lock):
            o = acc_ref[g]
            o2 = o * o
            s_all = jnp.sum(o2, axis=-1, keepdims=True)
            s_lo = jnp.sum(jnp.where(lane_lo, o2, 0.0), axis=-1, keepdims=True)
            inv = jnp.where(lane_lo, lax.rsqrt(s_lo / SB_HEAD_DIM + EPS),
                            lax.rsqrt((s_all - s_lo) / SB_HEAD_DIM + EPS))
            o_ref[0, pl.ds(q0s[g], mq), :] = (o * inv * gsb).astype(_BF16)
        return unused

    lax.fori_loop(0, n_super, superblock, 0)


def _tri_matrix():
    r = lax.broadcasted_iota(jnp.int32, (2 * KEY_BLOCK, 2 * KEY_BLOCK), 0) % KEY_BLOCK
    c = lax.broadcasted_iota(jnp.int32, (2 * KEY_BLOCK, 2 * KEY_BLOCK), 1)
    return jnp.where((c >= KEY_BLOCK) | (r >= c), 1.0, 0.0).astype(_BF16)


def _stick_breaking(q, k_new, v_new, k_past, v_past, g_sb, *, mq, g_lock, q_off, front_pad):
    b, t_new, _ = k_new.shape
    n_past = 0 if k_past is None else k_past.shape[1]
    n_super = (t_new - q_off) // (mq * g_lock)
    assert n_super * mq * g_lock == t_new - q_off
    last_end = front_pad + n_past + q_off + KEY_BLOCK + (n_super * g_lock - 1) * mq
    rows = max(pl.cdiv(front_pad + n_past + t_new, KEY_BLOCK) * KEY_BLOCK, last_end)
    pairs = SB_WIDTH // LANES
    new_spec = pl.BlockSpec((1, t_new, LANES), lambda i, p: (i, 0, p))
    in_specs = [new_spec, new_spec, new_spec]
    args = [q, k_new, v_new]
    if n_past:
        past_spec = pl.BlockSpec((1, n_past, LANES), lambda i, p: (i, 0, p))
        in_specs += [past_spec, past_spec]
        args += [k_past, v_past]
    in_specs += [pl.BlockSpec((1, LANES), lambda i, p: (0, p)),
                 pl.BlockSpec((2 * KEY_BLOCK, 2 * KEY_BLOCK), lambda i, p: (0, 0))]
    args += [g_sb, _tri_matrix()]
    return pl.pallas_call(
        functools.partial(_sb_kernel, mq=mq, g_lock=g_lock, n_super=n_super, q_off=q_off,
                          front_pad=front_pad, n_past=n_past),
        grid=(b, pairs),
        in_specs=in_specs,
        out_specs=pl.BlockSpec((1, t_new, LANES), lambda i, p: (i, 0, p)),
        out_shape=jax.ShapeDtypeStruct((b, t_new, SB_WIDTH), _BF16),
        scratch_shapes=[pltpu.VMEM((rows, LANES), _BF16)] * 3
        + [pltpu.VMEM((g_lock, mq, LANES), _F32), pltpu.VMEM((g_lock, 2 * mq, LANES), _F32)],
        compiler_params=pltpu.CompilerParams(
            dimension_semantics=("arbitrary", "arbitrary"), vmem_limit_bytes=VMEM_LIMIT),
        name=f"stick_breaking_t{t_new}",
    )(*args)


def _split_bf16(x):
    hi = x.astype(_BF16)
    return hi, (x - hi.astype(_F32)).astype(_BF16)


def _mix_kernel(a_ref, b_ref, x_ref, wa_ref, wb_ref, g_ref, wrh_ref, wrl_ref, br_ref,
                x2_ref, h2_ref, gates_ref, *, n, off):
    r = pl.program_id(1)
    a0 = pl.multiple_of(off + r * n, 16)
    x2 = (x_ref[...] + _dot(a_ref[0, pl.ds(a0, n), :], wa_ref[...])
          + _dot(b_ref[0, pl.ds(a0, n), :], wb_ref[...]))
    x2_ref[...] = x2
    hf = _rms(x2, g_ref[...])
    h_hi, h_lo = _split_bf16(hf)
    h2_ref[...] = h_hi
    nt = (((1,), (1,)), ((), ()))
    wrh = wrh_ref[...]
    logits = (lax.dot_general(wrh, h_hi, nt, preferred_element_type=_F32)
              + lax.dot_general(wrh, h_lo, nt, preferred_element_type=_F32)
              + lax.dot_general(wrl_ref[...], h_hi, nt, preferred_element_type=_F32))
    logits = logits[:ROUTER_ROWS] + jnp.tile(br_ref[...], (1, n // LANES))[:ROUTER_ROWS]
    row = lax.broadcasted_iota(jnp.int32, (ROUTER_ROWS, n), 0)
    neg = -jnp.inf
    big = 4 * LANES
    is_g = (row >= N_EXPERTS) & (row < N_EXPERTS + N_GROUPS)
    lg = jnp.where(is_g, logits, neg)
    g_max = jnp.max(lg, axis=0, keepdims=True)
    p_g_sel = 1.0 / jnp.sum(jnp.where(is_g, jnp.exp(lg - g_max), 0.0), axis=0, keepdims=True)
    g_sel = jnp.min(jnp.where(lg == g_max, row, big), axis=0, keepdims=True) - N_EXPERTS
    in_grp = (row < N_EXPERTS) & ((row // EXPERTS_PER_GROUP) == g_sel)
    le = jnp.where(in_grp, logits, neg)
    e_max = jnp.max(le, axis=0, keepdims=True)
    pe = jnp.where(in_grp, jnp.exp(le - e_max), 0.0)
    pe = pe / jnp.sum(pe, axis=0, keepdims=True)
    pe = jnp.where(in_grp, pe, -1.0)
    top1 = jnp.max(pe, axis=0, keepdims=True)
    i1 = jnp.min(jnp.where(pe == top1, row, big), axis=0, keepdims=True)
    pe_rest = jnp.where(row == i1, -1.0, pe)
    top2 = jnp.max(pe_rest, axis=0, keepdims=True)
    i2 = jnp.min(jnp.where(pe_rest == top2, row, big), axis=0, keepdims=True)
    denom = top1 + top2
    gates = jnp.where(row == i1, top1 / denom, jnp.where(row == i2, top2 / denom, 0.0)) * p_g_sel
    gates = jnp.where(row == GROUP_ROW, g_sel.astype(_F32), gates)
    gates_ref[...] = jnp.concatenate([gates, jnp.zeros((LANES - ROUTER_ROWS, n), _F32)], axis=0)


def _mix(a, b_out, x, wa, wb, g_ffn, wr_hi, wr_lo, br, *, n, off):
    bsz, t_a, _ = a.shape
    rows, d = x.shape
    t_x = rows // bsz
    assert t_x % n == 0 and n % LANES == 0
    steps = t_x // n
    res = pl.BlockSpec((1, t_a, CONV_CH), lambda i, r: (i, 0, 0))
    tile = pl.BlockSpec((n, d), lambda i, r: (i * steps + r, 0))
    full = lambda shape: pl.BlockSpec(shape, lambda i, r: (0,) * len(shape))
    return pl.pallas_call(
        functools.partial(_mix_kernel, n=n, off=off),
        grid=(bsz, steps),
        in_specs=[res, res, tile, full(wa.shape), full(wb.shape), full((1, d)),
                  full(wr_hi.shape), full(wr_lo.shape), full((LANES, LANES))],
        out_specs=[tile, tile, pl.BlockSpec((LANES, n), lambda i, r: (0, i * steps + r))],
        out_shape=[jax.ShapeDtypeStruct((rows, d), _F32),
                   jax.ShapeDtypeStruct((rows, d), _BF16),
                   jax.ShapeDtypeStruct((LANES, rows), _F32)],
        compiler_params=pltpu.CompilerParams(
            dimension_semantics=("arbitrary", "arbitrary"), vmem_limit_bytes=VMEM_LIMIT),
        name=f"mix_r{rows}",
    )(a, b_out, x, wa, wb, g_ffn, wr_hi, wr_lo, br)


def _moe_kernel(h_ref, gt_ref, x2_ref, win_a_ref, win_b_ref, wout_a_ref, wout_b_ref, gf_ref, y_ref,
                hs_ref, gs_ref, ys_ref, ut_ref, pos_ref, meta_ref, *, tm, tms_p, ffn_rows):
    i = pl.program_id(0)
    s = pl.program_id(1)
    steps_per_group = EXPERTS_PER_GROUP // EXPERT_STEP
    half = EXPERT_STEP // 2
    nt = (((1,), (1,)), ((), ()))
    tms = hs_ref.shape[0]

    @pl.when((i == 0) & (s == 0))
    def _():
        ut_ref[...] = jnp.where(lax.broadcasted_iota(jnp.int32, (tm, tm), 0)
                                < lax.broadcasted_iota(jnp.int32, (tm, tm), 1), 1.0, 0.0).astype(_BF16)

    @pl.when(s == 0)
    def _():
        gt = gt_ref[...]
        gid = gt[GROUP_ROW:GROUP_ROW + 1, :].astype(jnp.int32)
        grow = lax.broadcasted_iota(jnp.int32, (8, tm), 0)
        onehot = jnp.where(grow == gid, 1.0, 0.0)
        rank = _dot(onehot.astype(_BF16), ut_ref[...])
        off = jnp.int32(0)
        off_rows = jnp.zeros((8, tm), _F32)
        for g in range(N_GROUPS):
            cnt = jnp.sum(jnp.where(grow == g, onehot, 0.0)).astype(jnp.int32)
            meta_ref[g] = off
            meta_ref[N_GROUPS + g] = (cnt + ffn_rows - 1) // ffn_rows
            off_rows = jnp.where(grow == g, off.astype(_F32), off_rows)
            off = off + (cnt + GROUP_ALIGN - 1) // GROUP_ALIGN * GROUP_ALIGN
        pos = jnp.sum(onehot * (rank + off_rows), axis=0, keepdims=True)
        pos_ref[...] = jnp.transpose(jnp.broadcast_to(pos, (LANES, tm)))
        pos_i = pos.astype(jnp.int32)
        g_hi, g_lo = _split_bf16(gt)
        g_hl = jnp.concatenate([g_hi, g_lo], axis=0)
        h = h_ref[...]
        for c in range(tms_p // PERM_ROWS):
            prow = lax.broadcasted_iota(jnp.int32, (PERM_ROWS, tm), 0) + c * PERM_ROWS
            perm = jnp.where(prow == pos_i, 1.0, 0.0).astype(_BF16)
            rows = pl.ds(c * PERM_ROWS, PERM_ROWS)
            hs_ref[rows, :] = _dot(perm, h).astype(_BF16)
            g2 = lax.dot_general(perm, g_hl, nt, preferred_element_type=_F32)
            gs_ref[rows, :] = g2[:, :LANES] + g2[:, LANES:]
        hs_ref[tms_p:, :] = jnp.zeros((tms - tms_p, D_MODEL), _BF16)
        gs_ref[tms_p:, :] = jnp.zeros((tms - tms_p, LANES), _F32)
        ys_ref[...] = jnp.zeros_like(ys_ref)

    grp = s // steps_per_group
    off = meta_ref[grp]
    lane = lax.broadcasted_iota(jnp.int32, (ffn_rows, LANES), 1)

    def ffn(b, carry):
        rows = pl.ds(pl.multiple_of(off + b * ffn_rows, GROUP_ALIGN), ffn_rows)
        hb = hs_ref[rows, :]
        gsb = gs_ref[rows, :]
        acc = jnp.zeros((ffn_rows, D_MODEL), _F32)
        for j in range(EXPERT_STEP):
            win_ref, wout_ref = (win_a_ref, wout_a_ref) if j < half else (win_b_ref, wout_b_ref)
            gu = _dot(hb, win_ref[j % half])
            act = (jax.nn.silu(gu[:, :D_EXPERT]) * gu[:, D_EXPERT:]).astype(_BF16)
            gate = jnp.sum(jnp.where(lane == s * EXPERT_STEP + j, gsb, 0.0), axis=-1, keepdims=True)
            acc = acc + gate * _dot(act, wout_ref[j % half])
        ys_ref[rows, :] += acc
        return carry

    lax.fori_loop(0, meta_ref[N_GROUPS + grp], ffn, 0)

    @pl.when(s == pl.num_programs(1) - 1)
    def _():
        hs_ref[0:tms_p, :] = ys_ref[0:tms_p, :].astype(_BF16)
        for c in range(tm // PERM_ROWS):
            rows = pl.ds(c * PERM_ROWS, PERM_ROWS)
            pos_c = jnp.tile(pos_ref[rows, :], (1, tms_p // LANES)).astype(jnp.int32)
            unperm = jnp.where(lax.broadcasted_iota(jnp.int32, (PERM_ROWS, tms_p), 1) == pos_c,
                               1.0, 0.0).astype(_BF16)
            y_ref[rows, :] = _rms(x2_ref[rows, :] + _dot(unperm, hs_ref[0:tms_p, :]), gf_ref[...])


def _moe(h2, gates_t, x2, w_e_in, w_e_out, g_final, *, tm):
    rows, d = h2.shape
    assert rows % tm == 0 and tm % PERM_ROWS == 0
    tms_p = pl.cdiv(tm + N_GROUPS * (GROUP_ALIGN - 1), PERM_ROWS) * PERM_ROWS
    ffn_rows = pl.cdiv(tm * 9 // (8 * N_GROUPS), GROUP_ALIGN) * GROUP_ALIGN
    tms = tms_p + ffn_rows
    n_steps = N_EXPERTS // EXPERT_STEP
    half = EXPERT_STEP // 2
    tok = lambda w: pl.BlockSpec((tm, w), lambda i, s: (i, 0))
    return pl.pallas_call(
        functools.partial(_moe_kernel, tm=tm, tms_p=tms_p, ffn_rows=ffn_rows),
        grid=(rows // tm, n_steps),
        in_specs=[tok(d), pl.BlockSpec((LANES, tm), lambda i, s: (0, i)), tok(d),
                  pl.BlockSpec((half, d, 2 * D_EXPERT), lambda i, s: (2 * s, 0, 0)),
                  pl.BlockSpec((half, d, 2 * D_EXPERT), lambda i, s: (2 * s + 1, 0, 0)),
                  pl.BlockSpec((half, D_EXPERT, d), lambda i, s: (2 * s, 0, 0)),
                  pl.BlockSpec((half, D_EXPERT, d), lambda i, s: (2 * s + 1, 0, 0)),
                  pl.BlockSpec((1, d), lambda i, s: (0, 0))],
        out_specs=tok(d),
        out_shape=jax.ShapeDtypeStruct((rows, d), _F32),
        scratch_shapes=[pltpu.VMEM((tms, d), _BF16),
                        pltpu.VMEM((tms, LANES), _F32),
                        pltpu.VMEM((tms, d), _F32),
                        pltpu.VMEM((tm, tm), _BF16),
                        pltpu.VMEM((tm, LANES), _F32),
                        pltpu.SMEM((2 * N_GROUPS,), jnp.int32)],
        compiler_params=pltpu.CompilerParams(
            dimension_semantics=("arbitrary", "arbitrary"), vmem_limit_bytes=MOE_VMEM_LIMIT),
        name=f"moe_r{rows}",
    )(h2, gates_t, x2, w_e_in, w_e_in, w_e_out, w_e_out, g_final)


def kernel(x_prompt, x_sample, cache_k, cache_v, cache_conv, meta_tokens, g_mix, w_in, conv_w, conv_b,
           conv_ln_g, conv_ln_b, g_sb, w_out, g_ffn, w_router_group, b_router_group, w_router_expert,
           b_router_expert, w_expert_in, w_expert_out, g_final):
    depth = w_in.shape[0]
    assert depth == 1, "single-layer trunk"
    bp, tp, d = x_prompt.shape
    bs, ts, _ = x_sample.shape
    n_past = cache_k.shape[2]

    w_in_b = w_in[0].astype(_BF16)
    wa = w_out[0, :CONV_CH].astype(_BF16)
    wb = w_out[0, CONV_CH:].astype(_BF16)
    w_e_in = w_expert_in[0].astype(_BF16)
    w_e_out = w_expert_out[0].astype(_BF16)
    pad = LANES - N_EXPERTS - N_GROUPS
    wr = jnp.concatenate([w_router_expert[0].T, w_router_group[0].T, jnp.zeros((pad, d), _F32)], axis=0)
    wr_hi = wr.astype(_BF16)
    wr_lo = (wr - wr_hi.astype(_F32)).astype(_BF16)
    br = jnp.concatenate([b_router_expert[0], b_router_group[0], jnp.zeros((pad,), _F32)])
    br = jnp.broadcast_to(br[:, None], (LANES, LANES))
    g_mix2, g_ffn2, g_fin2, g_sb2 = g_mix[0][None], g_ffn[0][None], g_final[None], g_sb[0][None]
    cb, lg, lb = conv_b[0][None], conv_ln_g[0][None], conv_ln_b[0][None]

    def tail(a, b_out, x, *, n, off, tm):
        rows = x.shape[0] * x.shape[1]
        x2, h2, gates_t = _mix(a, b_out, x.reshape(rows, d), wa, wb, g_ffn2, wr_hi, wr_lo, br, n=n, off=off)
        return _moe(h2, gates_t, x2, w_e_in, w_e_out, g_fin2, tm=tm).reshape(x.shape)

    a_p, q_p, k_p, v_p, nc_p = _front(x_prompt, meta_tokens, g_mix2, w_in_b,
                                      jnp.zeros((bp, HIST_PAD, CONV_CH), _F32), conv_w[0], cb, lg, lb,
                                      n_tiles=3, chunk_rows=256)
    b_p = _stick_breaking(q_p, k_p, v_p, None, None, g_sb2, mq=KEY_BLOCK, g_lock=8, q_off=N_META,
                          front_pad=2 * KEY_BLOCK - N_META)
    y_p = tail(a_p, b_p, x_prompt, n=512, off=N_META, tm=1024)

    hist = jnp.pad(cache_conv[0], ((0, 0), (HIST_PAD - CONV_HIST, 0), (0, 0)))
    a_s, q_s, k_s, v_s, nc_s = _front(x_sample, None, g_mix2, w_in_b, hist, conv_w[0], cb, lg, lb,
                                      n_tiles=1, chunk_rows=ts)
    b_s = _stick_breaking(q_s, k_s, v_s, cache_k[0].reshape(bs, n_past, SB_WIDTH),
                          cache_v[0].reshape(bs, n_past, SB_WIDTH), g_sb2, mq=ts, g_lock=1, q_off=0,
                          front_pad=KEY_BLOCK)
    flat = lambda t: t.reshape(1, bs * ts, t.shape[-1])
    y_s = tail(flat(a_s), flat(b_s), flat(x_sample), n=bs * ts, off=0, tm=bs * ts).reshape(x_sample.shape)

    heads = (SB_HEADS, SB_HEAD_DIM)
    lead = HIST_PAD - CONV_HIST
    return (y_p, y_s,
            k_p.reshape(1, bp, tp + N_META, *heads), v_p.reshape(1, bp, tp + N_META, *heads),
            nc_p[None, :, lead:],
            k_s.reshape(1, bs, ts, *heads), v_s.reshape(1, bs, ts, *heads),
            nc_s[None, :, lead:])
```

```python
import functools

import jax
import jax.numpy as jnp
from jax import lax
from jax.experimental import pallas as pl
from jax.experimental.pallas import tpu as pltpu

D_MODEL = 1024
N_META = 16
CONV_CH = 512
CONV_WIDTH = 31
CONV_HIST = CONV_WIDTH - 1
SB_HEADS = 8
SB_HEAD_DIM = 64
SB_WIDTH = SB_HEADS * SB_HEAD_DIM
N_GROUPS = 4
EXPERTS_PER_GROUP = 8
N_EXPERTS = N_GROUPS * EXPERTS_PER_GROUP
D_EXPERT = 256
EPS = 1e-6

LANES = 128
HIST_PAD = 32
KEY_BLOCK = 128
KEY_CHUNK = 256
LOG2_E = 1.4426950408889634
MASKED_Z = -1e30
EXIT_LOG2 = 128.0
ROUTER_ROWS = 40
GROUP_ROW = N_EXPERTS
GROUP_ALIGN = 16
PERM_ROWS = 256
EXPERT_STEP = 4
VMEM_LIMIT = 52 * 1024 * 1024
MOE_VMEM_LIMIT = 58 * 1024 * 1024

_F32 = jnp.float32
_BF16 = jnp.bfloat16


def _dot(a, b):
    return jnp.dot(a, b, preferred_element_type=_F32)


def _rms(x, g):
    return x * lax.rsqrt(jnp.mean(x * x, axis=-1, keepdims=True) + EPS) * g


def _conv_chunk(win, cwb_ref, rows):
    lead = HIST_PAD - CONV_HIST
    tiles = rows // 8
    acc = jnp.zeros((tiles, 8, CONV_CH), _F32)
    for s in range(8):
        taps = [k for k in range(CONV_WIDTH) if (lead + k) % 8 == s]
        shifted = win if s == 0 else win[s:s + rows + HIST_PAD - 8, :]
        for k in taps:
            off = lead + k - s
            tap = shifted[off:off + rows, :].reshape(tiles, 8, CONV_CH)
            acc = acc + cwb_ref[8 * k:8 * k + 8, :][None] * tap
    return acc.reshape(rows, CONV_CH)


def _front_kernel(*refs, tile, has_meta, chunks):
    if has_meta:
        x_ref, meta_ref = refs[:2]
        refs = refs[2:]
    else:
        x_ref = refs[0]
        refs = refs[1:]
    (g_ref, w_ref, hist_ref, cw_ref, cb_ref, lg_ref, lb_ref,
     a_ref, q_ref, k_ref, v_ref, nc_ref, h_ref, uh_ref, cwb_ref) = refs
    t = pl.program_id(1)
    g = g_ref[...]
    if has_meta:
        main0 = pl.multiple_of(t * tile, 8)
        h_ref[N_META:, :] = _rms(x_ref[0, pl.ds(main0, tile - N_META), :], g).astype(_BF16)
        head0 = pl.multiple_of(jnp.maximum(t * tile - N_META, 0), 8)
        xh = jnp.where(t == 0, meta_ref[...], x_ref[0, pl.ds(head0, N_META), :])
        h_ref[0:N_META, :] = _rms(xh, g).astype(_BF16)
    else:
        h_ref[...] = _rms(x_ref[0], g).astype(_BF16)

    @pl.when(t == 0)
    def _():
        uh_ref[0:HIST_PAD, :] = hist_ref[0]

    for k in range(CONV_WIDTH):
        cwb_ref[8 * k:8 * k + 8, :] = jnp.broadcast_to(cw_ref[k:k + 1, :], (8, CONV_CH))

    h = h_ref[...]
    c, s = CONV_CH, SB_WIDTH
    uh_ref[HIST_PAD:, :] = _dot(h, w_ref[:, 0:c]) * jax.nn.sigmoid(_dot(h, w_ref[:, c:2 * c]))
    q_ref[0] = (_dot(h, w_ref[:, 2 * c:2 * c + s]) * (SB_HEAD_DIM ** -0.5 * LOG2_E)).astype(_BF16)
    k_ref[0] = _dot(h, w_ref[:, 2 * c + s:2 * c + 2 * s])
    v_ref[0] = _dot(h, w_ref[:, 2 * c + 2 * s:2 * c + 3 * s])

    cb, lg, lb = cb_ref[...], lg_ref[...], lb_ref[...]
    for r0, rows in chunks:
        acc = _conv_chunk(uh_ref[r0:r0 + rows + HIST_PAD, :], cwb_ref, rows) + cb
        mu = jnp.mean(acc, axis=-1, keepdims=True)
        dlt = acc - mu
        var = jnp.mean(dlt * dlt, axis=-1, keepdims=True)
        y = dlt * lax.rsqrt(var + EPS) * lg + lb
        a_ref[0, r0:r0 + rows, :] = (y * jax.nn.sigmoid(y)).astype(_BF16)

    last = uh_ref[tile:tile + HIST_PAD, :]
    nc_ref[0] = last
    uh_ref[0:HIST_PAD, :] = last


def _front(x, meta, g_mix, w_in, hist_padded, conv_w, conv_b, ln_g, ln_b, *, n_tiles, chunk_rows):
    b, t_x, d = x.shape
    t_all = t_x + (N_META if meta is not None else 0)
    tile = t_all // n_tiles
    assert tile * n_tiles == t_all and tile % 16 == 0 and tile >= HIST_PAD
    chunks = tuple((r0, min(chunk_rows, tile - r0)) for r0 in range(0, tile, chunk_rows))
    assert all(rows % 16 == 0 for _, rows in chunks)
    in_w = w_in.shape[1]
    const = lambda shape: pl.BlockSpec(shape, lambda i, j: (0,) * len(shape))
    in_specs = [pl.BlockSpec((1, t_x, d), lambda i, j: (i, 0, 0))]
    args = [x]
    if meta is not None:
        in_specs.append(const((N_META, d)))
        args.append(meta)
    in_specs += [const((1, d)), const((d, in_w)), pl.BlockSpec((1, HIST_PAD, CONV_CH), lambda i, j: (i, 0, 0)),
                 const((CONV_WIDTH, CONV_CH)), const((1, CONV_CH)), const((1, CONV_CH)), const((1, CONV_CH))]
    args += [g_mix, w_in, hist_padded, conv_w, conv_b, ln_g, ln_b]
    row_spec = pl.BlockSpec((1, tile, CONV_CH), lambda i, j: (i, j, 0))
    return pl.pallas_call(
        functools.partial(_front_kernel, tile=tile, has_meta=meta is not None, chunks=chunks),
        grid=(b, n_tiles),
        in_specs=in_specs,
        out_specs=[row_spec] * 4 + [pl.BlockSpec((1, HIST_PAD, CONV_CH), lambda i, j: (i, 0, 0))],
        out_shape=[jax.ShapeDtypeStruct((b, t_all, CONV_CH), _BF16),
                   jax.ShapeDtypeStruct((b, t_all, SB_WIDTH), _BF16),
                   jax.ShapeDtypeStruct((b, t_all, SB_WIDTH), _F32),
                   jax.ShapeDtypeStruct((b, t_all, SB_WIDTH), _F32),
                   jax.ShapeDtypeStruct((b, HIST_PAD, CONV_CH), _F32)],
        scratch_shapes=[pltpu.VMEM((tile, d), _BF16),
                        pltpu.VMEM((HIST_PAD + tile, CONV_CH), _F32),
                        pltpu.VMEM((8 * CONV_WIDTH, CONV_CH), _F32)],
        compiler_params=pltpu.CompilerParams(
            dimension_semantics=("arbitrary", "arbitrary"), vmem_limit_bytes=VMEM_LIMIT),
        name=f"front_t{t_all}",
    )(*args)


def _sb_kernel(*refs, mq, g_lock, n_super, q_off, front_pad, n_past):
    if n_past:
        q_ref, kn_ref, vn_ref, kp_ref, vp_ref, g_ref, tri_ref, o_ref = refs[:8]
    else:
        q_ref, kn_ref, vn_ref, g_ref, tri_ref, o_ref = refs[:6]
    kb_ref, v0_ref, v1_ref, acc_ref, carry_ref = refs[-5:]
    t_new = kn_ref.shape[1]
    rows = kb_ref.shape[0]
    lane_row = lax.broadcasted_iota(jnp.int32, (1, LANES), 1) < SB_HEAD_DIM

    def fill(r0, n, k, v):
        kb_ref[r0:r0 + n, :] = k.astype(_BF16)
        v0_ref[r0:r0 + n, :] = jnp.where(lane_row, v, 0.0).astype(_BF16)
        v1_ref[r0:r0 + n, :] = jnp.where(lane_row, 0.0, v).astype(_BF16)

    new0 = front_pad + n_past
    if front_pad:
        fill(0, front_pad, jnp.zeros((front_pad, LANES), _F32), jnp.zeros((front_pad, LANES), _F32))
    if n_past:
        fill(front_pad, n_past, kp_ref[0], vp_ref[0])
    fill(new0, t_new, kn_ref[0], vn_ref[0])
    tail = rows - new0 - t_new
    if tail:
        fill(new0 + t_new, tail, jnp.zeros((tail, LANES), _F32), jnp.zeros((tail, LANES), _F32))
    if q_off:
        o_ref[0, 0:q_off, :] = jnp.zeros((q_off, LANES), _BF16)

    tri = tri_ref[...]
    gsb = g_ref[...]
    lane_lo = lax.broadcasted_iota(jnp.int32, (mq, LANES), 1) < SB_HEAD_DIM
    col_minus_row = (lax.broadcasted_iota(jnp.int32, (2 * mq, KEY_BLOCK), 1) + KEY_BLOCK
                     - lax.broadcasted_iota(jnp.int32, (2 * mq, KEY_BLOCK), 0) % mq)
    assert (new0 + q_off) % KEY_BLOCK == 0 and front_pad >= KEY_BLOCK
    assert mq == KEY_BLOCK or g_lock * n_super == 1
    end0 = new0 + q_off + KEY_BLOCK
    nt_dims = (((1,), (1,)), ((), ()))

    def superblock(sb, unused):
        i0 = sb * g_lock
        q0s = [pl.multiple_of(q_off + (i0 + g) * mq, 16) for g in range(g_lock)]
        qhs = []
        for g in range(g_lock):
            qb = q_ref[0, pl.ds(q0s[g], mq), :]
            zero_q = jnp.zeros_like(qb)
            qhs.append(jnp.concatenate([jnp.where(lane_lo, qb, zero_q), jnp.where(lane_lo, zero_q, qb)],
                                       axis=0))
        acc_ref[...] = jnp.zeros_like(acc_ref)
        carry_ref[...] = jnp.zeros_like(carry_ref)

        def sweep(state):
            n = state[0]
            low = jnp.full((1, LANES), jnp.inf, _F32)
            starts = [end0 + (i0 + g) * mq - (n + 1) * KEY_CHUNK for g in range(g_lock)]
            r0s = [pl.multiple_of(jnp.maximum(st, 0), KEY_BLOCK) for st in starts]
            zs, cts = [], []
            for g in range(g_lock):
                z = lax.dot_general(qhs[g], kb_ref[pl.ds(r0s[g], KEY_CHUNK), :], nt_dims,
                                    preferred_element_type=_F32)
                hidden_r = jnp.where(col_minus_row < KEY_BLOCK + n * KEY_CHUNK, z[:, KEY_BLOCK:], MASKED_Z)
                zs.append(jnp.concatenate([z[:, :KEY_BLOCK], hidden_r], axis=1))
            for g in range(g_lock):
                z = zs[g]
                s = jnp.maximum(z, 0.0) + jnp.log2(1.0 + jnp.exp2(-jnp.abs(z)))
                hi = s.astype(_BF16)
                lo = (s - hi.astype(_F32)).astype(_BF16)
                cts.append(_dot(jnp.concatenate(
                    [jnp.concatenate([hi[:, KEY_BLOCK:], lo[:, KEY_BLOCK:]], axis=1),
                     jnp.concatenate([hi[:, :KEY_BLOCK], lo[:, :KEY_BLOCK]], axis=1)], axis=0), tri))
            for g in range(g_lock):
                ct_r, ct_l = cts[g][:2 * mq], cts[g][2 * mq:]
                carry = carry_ref[g]
                c_r = ct_r[:, :KEY_BLOCK] + carry
                carry = carry + ct_r[:, KEY_BLOCK:]
                c_l = ct_l[:, :KEY_BLOCK] + carry
                carry = carry + ct_l[:, KEY_BLOCK:]
                carry_ref[g] = carry
                w = jnp.exp2(zs[g] - jnp.concatenate([c_l, c_r], axis=1)).astype(_BF16)
                pv = (_dot(w[:mq], v0_ref[pl.ds(r0s[g], KEY_CHUNK), :])
                      + _dot(w[mq:], v1_ref[pl.ds(r0s[g], KEY_CHUNK), :]))
                acc_ref[g] += jnp.where(starts[g] >= 0, pv, 0.0)
                done = jnp.where(starts[g] - KEY_CHUNK >= 0, 0.0, 2 * EXIT_LOG2)
                low = jnp.minimum(low, jnp.min(carry, axis=0, keepdims=True) + done)
            return n + 1, jnp.where(jnp.min(low) > EXIT_LOG2, 0, 1)

        lax.while_loop(lambda st: st[1] > 0, sweep, (jnp.int32(0), jnp.int32(1)))

        for g in range(g_lock):
            o = acc_ref[g]
            o2 = o * o
            s_all = jnp.sum(o2, axis=-1, keepdims=True)
            s_lo = jnp.sum(jnp.where(lane_lo, o2, 0.0), axis=-1, keepdims=True)
            inv = jnp.where(lane_lo, lax.rsqrt(s_lo / SB_HEAD_DIM + EPS),
                            lax.rsqrt((s_all - s_lo) / SB_HEAD_DIM + EPS))
            o_ref[0, pl.ds(q0s[g], mq), :] = (o * inv * gsb).astype(_BF16)
        return unused

    lax.fori_loop(0, n_super, superblock, 0)


def _tri_matrix():
    r = lax.broadcasted_iota(jnp.int32, (2 * KEY_BLOCK, 2 * KEY_BLOCK), 0) % KEY_BLOCK
    c = lax.broadcasted_iota(jnp.int32, (2 * KEY_BLOCK, 2 * KEY_BLOCK), 1)
    return jnp.where((c >= KEY_BLOCK) | (r >= c), 1.0, 0.0).astype(_BF16)


def _stick_breaking(q, k_new, v_new, k_past, v_past, g_sb, *, mq, g_lock, q_off, front_pad):
    b, t_new, _ = k_new.shape
    n_past = 0 if k_past is None else k_past.shape[1]
    n_super = (t_new - q_off) // (mq * g_lock)
    assert n_super * mq * g_lock == t_new - q_off
    last_end = front_pad + n_past + q_off + KEY_BLOCK + (n_super * g_lock - 1) * mq
    rows = max(pl.cdiv(front_pad + n_past + t_new, KEY_BLOCK) * KEY_BLOCK, last_end)
    pairs = SB_WIDTH // LANES
    new_spec = pl.BlockSpec((1, t_new, LANES), lambda i, p: (i, 0, p))
    in_specs = [new_spec, new_spec, new_spec]
    args = [q, k_new, v_new]
    if n_past:
        past_spec = pl.BlockSpec((1, n_past, LANES), lambda i, p: (i, 0, p))
        in_specs += [past_spec, past_spec]
        args += [k_past, v_past]
    in_specs += [pl.BlockSpec((1, LANES), lambda i, p: (0, p)),
                 pl.BlockSpec((2 * KEY_BLOCK, 2 * KEY_BLOCK), lambda i, p: (0, 0))]
    args += [g_sb, _tri_matrix()]
    return pl.pallas_call(
        functools.partial(_sb_kernel, mq=mq, g_lock=g_lock, n_super=n_super, q_off=q_off,
                          front_pad=front_pad, n_past=n_past),
        grid=(b, pairs),
        in_specs=in_specs,
        out_specs=pl.BlockSpec((1, t_new, LANES), lambda i, p: (i, 0, p)),
        out_shape=jax.ShapeDtypeStruct((b, t_new, SB_WIDTH), _BF16),
        scratch_shapes=[pltpu.VMEM((rows, LANES), _BF16)] * 3
        + [pltpu.VMEM((g_lock, mq, LANES), _F32), pltpu.VMEM((g_lock, 2 * mq, LANES), _F32)],
        compiler_params=pltpu.CompilerParams(
            dimension_semantics=("arbitrary", "arbitrary"), vmem_limit_bytes=VMEM_LIMIT),
        name=f"stick_breaking_t{t_new}",
    )(*args)


def _split_bf16(x):
    hi = x.astype(_BF16)
    return hi, (x - hi.astype(_F32)).astype(_BF16)


def _route(logits):
    n = logits.shape[1]
    row = lax.broadcasted_iota(jnp.int32, (ROUTER_ROWS, n), 0)
    neg = -jnp.inf
    big = 4 * LANES
    is_g = (row >= N_EXPERTS) & (row < N_EXPERTS + N_GROUPS)
    lg = jnp.where(is_g, logits, neg)
    g_max = jnp.max(lg, axis=0, keepdims=True)
    p_g_sel = 1.0 / jnp.sum(jnp.where(is_g, jnp.exp(lg - g_max), 0.0), axis=0, keepdims=True)
    g_sel = jnp.min(jnp.where(lg == g_max, row, big), axis=0, keepdims=True) - N_EXPERTS
    in_grp = (row < N_EXPERTS) & ((row // EXPERTS_PER_GROUP) == g_sel)
    le = jnp.where(in_grp, logits, neg)
    e_max = jnp.max(le, axis=0, keepdims=True)
    pe = jnp.where(in_grp, jnp.exp(le - e_max), 0.0)
    pe = pe / jnp.sum(pe, axis=0, keepdims=True)
    pe = jnp.where(in_grp, pe, -1.0)
    top1 = jnp.max(pe, axis=0, keepdims=True)
    i1 = jnp.min(jnp.where(pe == top1, row, big), axis=0, keepdims=True)
    pe_rest = jnp.where(row == i1, -1.0, pe)
    top2 = jnp.max(pe_rest, axis=0, keepdims=True)
    i2 = jnp.min(jnp.where(pe_rest == top2, row, big), axis=0, keepdims=True)
    denom = top1 + top2
    gates = jnp.where(row == i1, top1 / denom, jnp.where(row == i2, top2 / denom, 0.0)) * p_g_sel
    return jnp.where(row == GROUP_ROW, g_sel.astype(_F32), gates)


def _tail_kernel(a_ref, b_ref, x_ref, wa_ref, wb_ref, gffn_ref, wrh_ref, wrl_ref, br_ref,
                 win_ref, wout_ref, gf_ref, y_ref,
                 h_ref, lg_ref, hs_ref, gs_ref, ys_ref, ut_ref, pos_ref, meta_ref,
                 *, tm, tms_p, ffn_rows, off, tiles_per_batch):
    i = pl.program_id(0)
    s = pl.program_id(1)
    steps_per_group = EXPERTS_PER_GROUP // EXPERT_STEP
    nt = (((1,), (1,)), ((), ()))
    tms = hs_ref.shape[0]

    @pl.when((i == 0) & (s == 0))
    def _():
        ut_ref[...] = jnp.where(lax.broadcasted_iota(jnp.int32, (tm, tm), 0)
                                < lax.broadcasted_iota(jnp.int32, (tm, tm), 1), 1.0, 0.0).astype(_BF16)

    @pl.when(s == 0)
    def _():
        a0 = off + (i % tiles_per_batch) * tm
        wrh = wrh_ref[...]
        for c in range(tm // PERM_ROWS):
            rows = pl.ds(c * PERM_ROWS, PERM_ROWS)
            arows = pl.ds(pl.multiple_of(a0 + c * PERM_ROWS, 16), PERM_ROWS)
            x2 = (x_ref[rows, :] + _dot(a_ref[0, arows, :], wa_ref[...])
                  + _dot(b_ref[0, arows, :], wb_ref[...]))
            y_ref[rows, :] = x2
            h_hi, h_lo = _split_bf16(_rms(x2, gffn_ref[...]))
            h_ref[rows, :] = h_hi
            lg_ref[:, c * PERM_ROWS:(c + 1) * PERM_ROWS] = (
                lax.dot_general(wrh, h_hi, nt, preferred_element_type=_F32)
                + lax.dot_general(wrh, h_lo, nt, preferred_element_type=_F32)
                + lax.dot_general(wrl_ref[...], h_hi, nt, preferred_element_type=_F32))
        gates = _route(lg_ref[0:ROUTER_ROWS, :] + jnp.tile(br_ref[...], (1, tm // LANES))[:ROUTER_ROWS])
        gt = jnp.concatenate([gates, jnp.zeros((LANES - ROUTER_ROWS, tm), _F32)], axis=0)

        gid = gt[GROUP_ROW:GROUP_ROW + 1, :].astype(jnp.int32)
        grow = lax.broadcasted_iota(jnp.int32, (8, tm), 0)
        onehot = jnp.where(grow == gid, 1.0, 0.0)
        rank = _dot(onehot.astype(_BF16), ut_ref[...])
        goff = jnp.int32(0)
        off_rows = jnp.zeros((8, tm), _F32)
        for g in range(N_GROUPS):
            cnt = jnp.sum(jnp.where(grow == g, onehot, 0.0)).astype(jnp.int32)
            meta_ref[g] = goff
            meta_ref[N_GROUPS + g] = (cnt + ffn_rows - 1) // ffn_rows
            off_rows = jnp.where(grow == g, goff.astype(_F32), off_rows)
            goff = goff + (cnt + GROUP_ALIGN - 1) // GROUP_ALIGN * GROUP_ALIGN
        pos = jnp.sum(onehot * (rank + off_rows), axis=0, keepdims=True)
        pos_ref[...] = jnp.transpose(jnp.broadcast_to(pos, (LANES, tm)))
        pos_i = pos.astype(jnp.int32)
        g_hi, g_lo = _split_bf16(gt)
        g_hl = jnp.concatenate([g_hi, g_lo], axis=0)
        h = h_ref[...]
        for c in range(tms_p // PERM_ROWS):
            prow = lax.broadcasted_iota(jnp.int32, (PERM_ROWS, tm), 0) + c * PERM_ROWS
            perm = jnp.where(prow == pos_i, 1.0, 0.0).astype(_BF16)
            rows = pl.ds(c * PERM_ROWS, PERM_ROWS)
            hs_ref[rows, :] = _dot(perm, h).astype(_BF16)
            g2 = lax.dot_general(perm, g_hl, nt, preferred_element_type=_F32)
            gs_ref[rows, :] = g2[:, :LANES] + g2[:, LANES:]
        hs_ref[tms_p:, :] = jnp.zeros((tms - tms_p, D_MODEL), _BF16)
        gs_ref[tms_p:, :] = jnp.zeros((tms - tms_p, LANES), _F32)
        ys_ref[...] = jnp.zeros_like(ys_ref)

    grp = s // steps_per_group
    goff = meta_ref[grp]
    lane = lax.broadcasted_iota(jnp.int32, (ffn_rows, LANES), 1)

    def ffn(b, carry):
        rows = pl.ds(pl.multiple_of(goff + b * ffn_rows, GROUP_ALIGN), ffn_rows)
        hb = hs_ref[rows, :]
        gsb = gs_ref[rows, :]
        acc = jnp.zeros((ffn_rows, D_MODEL), _F32)
        for j in range(EXPERT_STEP):
            gu = _dot(hb, win_ref[j])
            act = (jax.nn.silu(gu[:, :D_EXPERT]) * gu[:, D_EXPERT:]).astype(_BF16)
            gate = jnp.sum(jnp.where(lane == s * EXPERT_STEP + j, gsb, 0.0), axis=-1, keepdims=True)
            acc = acc + gate * _dot(act, wout_ref[j])
        ys_ref[rows, :] += acc
        return carry

    lax.fori_loop(0, meta_ref[N_GROUPS + grp], ffn, 0)

    @pl.when(s == pl.num_programs(1) - 1)
    def _():
        hs_ref[0:tms_p, :] = ys_ref[0:tms_p, :].astype(_BF16)
        for c in range(tm // PERM_ROWS):
            rows = pl.ds(c * PERM_ROWS, PERM_ROWS)
            pos_c = jnp.tile(pos_ref[rows, :], (1, tms_p // LANES)).astype(jnp.int32)
            unperm = jnp.where(lax.broadcasted_iota(jnp.int32, (PERM_ROWS, tms_p), 1) == pos_c,
                               1.0, 0.0).astype(_BF16)
            y_ref[rows, :] = _rms(y_ref[rows, :] + _dot(unperm, hs_ref[0:tms_p, :]), gf_ref[...])


def _tail(a, b_out, x, wa, wb, g_ffn, wr_hi, wr_lo, br, w_e_in, w_e_out, g_final, *, tm, off):
    bsz, t_a, _ = a.shape
    rows, d = x.shape
    t_x = rows // bsz
    assert t_x % tm == 0 and tm % PERM_ROWS == 0
    tiles_per_batch = t_x // tm
    tms_p = pl.cdiv(tm + N_GROUPS * (GROUP_ALIGN - 1), PERM_ROWS) * PERM_ROWS
    ffn_rows = pl.cdiv(tm * 9 // (8 * N_GROUPS), GROUP_ALIGN) * GROUP_ALIGN
    tms = tms_p + ffn_rows
    n_steps = N_EXPERTS // EXPERT_STEP
    res = pl.BlockSpec((1, t_a, CONV_CH), lambda i, s: (i // tiles_per_batch, 0, 0))
    tok = pl.BlockSpec((tm, d), lambda i, s: (i, 0))
    const = lambda shape: pl.BlockSpec(shape, lambda i, s: (0,) * len(shape), pipeline_mode=pl.Buffered(1))
    return pl.pallas_call(
        functools.partial(_tail_kernel, tm=tm, tms_p=tms_p, ffn_rows=ffn_rows, off=off,
                          tiles_per_batch=tiles_per_batch),
        grid=(rows // tm, n_steps),
        in_specs=[res, res, tok, const(wa.shape), const(wb.shape), const((1, d)),
                  const(wr_hi.shape), const(wr_lo.shape), const((LANES, LANES)),
                  pl.BlockSpec((EXPERT_STEP, d, 2 * D_EXPERT), lambda i, s: (s, 0, 0)),
                  pl.BlockSpec((EXPERT_STEP, D_EXPERT, d), lambda i, s: (s, 0, 0)),
                  const((1, d))],
        out_specs=tok,
        out_shape=jax.ShapeDtypeStruct((rows, d), _F32),
        scratch_shapes=[pltpu.VMEM((tm, d), _BF16),
                        pltpu.VMEM((LANES, tm), _F32),
                        pltpu.VMEM((tms, d), _BF16),
                        pltpu.VMEM((tms, LANES), _F32),
                        pltpu.VMEM((tms, d), _F32),
                        pltpu.VMEM((tm, tm), _BF16),
                        pltpu.VMEM((tm, LANES), _F32),
                        pltpu.SMEM((2 * N_GROUPS,), jnp.int32)],
        compiler_params=pltpu.CompilerParams(
            dimension_semantics=("arbitrary", "arbitrary"), vmem_limit_bytes=MOE_VMEM_LIMIT),
        name=f"tail_r{rows}",
    )(a, b_out, x, wa, wb, g_ffn, wr_hi, wr_lo, br, w_e_in, w_e_out, g_final)


def kernel(x_prompt, x_sample, cache_k, cache_v, cache_conv, meta_tokens, g_mix, w_in, conv_w, conv_b,
           conv_ln_g, conv_ln_b, g_sb, w_out, g_ffn, w_router_group, b_router_group, w_router_expert,
           b_router_expert, w_expert_in, w_expert_out, g_final):
    depth = w_in.shape[0]
    assert depth == 1, "single-layer trunk"
    bp, tp, d = x_prompt.shape
    bs, ts, _ = x_sample.shape
    n_past = cache_k.shape[2]

    w_in_b = w_in[0].astype(_BF16)
    wa = w_out[0, :CONV_CH].astype(_BF16)
    wb = w_out[0, CONV_CH:].astype(_BF16)
    w_e_in = w_expert_in[0].astype(_BF16)
    w_e_out = w_expert_out[0].astype(_BF16)
    pad = LANES - N_EXPERTS - N_GROUPS
    wr = jnp.concatenate([w_router_expert[0].T, w_router_group[0].T, jnp.zeros((pad, d), _F32)], axis=0)
    wr_hi = wr.astype(_BF16)
    wr_lo = (wr - wr_hi.astype(_F32)).astype(_BF16)
    br = jnp.concatenate([b_router_expert[0], b_router_group[0], jnp.zeros((pad,), _F32)])
    br = jnp.broadcast_to(br[:, None], (LANES, LANES))
    g_mix2, g_ffn2, g_fin2, g_sb2 = g_mix[0][None], g_ffn[0][None], g_final[None], g_sb[0][None]
    cb, lg, lb = conv_b[0][None], conv_ln_g[0][None], conv_ln_b[0][None]

    def tail(a, b_out, x, *, off, tm):
        rows = x.shape[0] * x.shape[1]
        return _tail(a, b_out, x.reshape(rows, d), wa, wb, g_ffn2, wr_hi, wr_lo, br, w_e_in, w_e_out, g_fin2,
                     tm=tm, off=off).reshape(x.shape)

    a_p, q_p, k_p, v_p, nc_p = _front(x_prompt, meta_tokens, g_mix2, w_in_b,
                                      jnp.zeros((bp, HIST_PAD, CONV_CH), _F32), conv_w[0], cb, lg, lb,
                                      n_tiles=3, chunk_rows=256)
    b_p = _stick_breaking(q_p, k_p, v_p, None, None, g_sb2, mq=KEY_BLOCK, g_lock=8, q_off=N_META,
                          front_pad=2 * KEY_BLOCK - N_META)
    y_p = tail(a_p, b_p, x_prompt, off=N_META, tm=1024)

    hist = jnp.pad(cache_conv[0], ((0, 0), (HIST_PAD - CONV_HIST, 0), (0, 0)))
    a_s, q_s, k_s, v_s, nc_s = _front(x_sample, None, g_mix2, w_in_b, hist, conv_w[0], cb, lg, lb,
                                      n_tiles=1, chunk_rows=ts)
    b_s = _stick_breaking(q_s, k_s, v_s, cache_k[0].reshape(bs, n_past, SB_WIDTH),
                          cache_v[0].reshape(bs, n_past, SB_WIDTH), g_sb2, mq=ts, g_lock=1, q_off=0,
                          front_pad=KEY_BLOCK)
    flat = lambda t: t.reshape(1, bs * ts, t.shape[-1])
    y_s = tail(flat(a_s), flat(b_s), flat(x_sample), off=0, tm=bs * ts).reshape(x_sample.shape)

    heads = (SB_HEADS, SB_HEAD_DIM)
    lead = HIST_PAD - CONV_HIST
    return (y_p, y_s,
            k_p.reshape(1, bp, tp + N_META, *heads), v_p.reshape(1, bp, tp + N_META, *heads),
            nc_p[None, :, lead:],
            k_s.reshape(1, bs, ts, *heads), v_s.reshape(1, bs, ts, *heads),
            nc_s[None, :, lead:])
```

```python
import functools

import jax
import jax.numpy as jnp
from jax import lax
from jax.experimental import pallas as pl
from jax.experimental.pallas import tpu as pltpu

D_MODEL = 1024
N_META = 16
CONV_CH = 512
CONV_WIDTH = 31
CONV_HIST = CONV_WIDTH - 1
SB_HEADS = 8
SB_HEAD_DIM = 64
SB_WIDTH = SB_HEADS * SB_HEAD_DIM
N_GROUPS = 4
EXPERTS_PER_GROUP = 8
N_EXPERTS = N_GROUPS * EXPERTS_PER_GROUP
D_EXPERT = 256
EPS = 1e-6

LANES = 128
HIST_PAD = 32
KEY_BLOCK = 128
KEY_CHUNK = 256
LOG2_E = 1.4426950408889634
MASKED_Z = -1e30
EXIT_LOG2 = 128.0
ROUTER_ROWS = 40
GROUP_ROW = N_EXPERTS
GROUP_ALIGN = 16
PERM_ROWS = 256
EXPERT_STEP = 4
VMEM_LIMIT = 52 * 1024 * 1024
MOE_VMEM_LIMIT = 58 * 1024 * 1024

_F32 = jnp.float32
_BF16 = jnp.bfloat16


def _dot(a, b):
    return jnp.dot(a, b, preferred_element_type=_F32)


def _rms(x, g):
    return x * lax.rsqrt(jnp.mean(x * x, axis=-1, keepdims=True) + EPS) * g


def _conv_chunk(win, cwb_ref, rows):
    lead = HIST_PAD - CONV_HIST
    tiles = rows // 8
    acc = jnp.zeros((tiles, 8, CONV_CH), _F32)
    for s in range(8):
        taps = [k for k in range(CONV_WIDTH) if (lead + k) % 8 == s]
        shifted = win if s == 0 else win[s:s + rows + HIST_PAD - 8, :]
        for k in taps:
            off = lead + k - s
            tap = shifted[off:off + rows, :].reshape(tiles, 8, CONV_CH)
            acc = acc + cwb_ref[8 * k:8 * k + 8, :][None] * tap
    return acc.reshape(rows, CONV_CH)


def _front_kernel(*refs, tile, has_meta, chunks):
    if has_meta:
        x_ref, meta_ref = refs[:2]
        refs = refs[2:]
    else:
        x_ref = refs[0]
        refs = refs[1:]
    (g_ref, w_ref, hist_ref, cw_ref, cb_ref, lg_ref, lb_ref,
     a_ref, q_ref, k_ref, v_ref, nc_ref, h_ref, uh_ref, cwb_ref) = refs
    t = pl.program_id(1)
    g = g_ref[...]
    if has_meta:
        main0 = pl.multiple_of(t * tile, 8)
        h_ref[N_META:, :] = _rms(x_ref[0, pl.ds(main0, tile - N_META), :], g).astype(_BF16)
        head0 = pl.multiple_of(jnp.maximum(t * tile - N_META, 0), 8)
        xh = jnp.where(t == 0, meta_ref[...], x_ref[0, pl.ds(head0, N_META), :])
        h_ref[0:N_META, :] = _rms(xh, g).astype(_BF16)
    else:
        h_ref[...] = _rms(x_ref[0], g).astype(_BF16)

    @pl.when(t == 0)
    def _():
        uh_ref[0:HIST_PAD, :] = hist_ref[0]

    for k in range(CONV_WIDTH):
        cwb_ref[8 * k:8 * k + 8, :] = jnp.broadcast_to(cw_ref[k:k + 1, :], (8, CONV_CH))

    h = h_ref[...]
    c, s = CONV_CH, SB_WIDTH
    uh_ref[HIST_PAD:, :] = _dot(h, w_ref[:, 0:c]) * jax.nn.sigmoid(_dot(h, w_ref[:, c:2 * c]))
    q_ref[0] = (_dot(h, w_ref[:, 2 * c:2 * c + s]) * (SB_HEAD_DIM ** -0.5 * LOG2_E)).astype(_BF16)
    k_ref[0] = _dot(h, w_ref[:, 2 * c + s:2 * c + 2 * s])
    v_ref[0] = _dot(h, w_ref[:, 2 * c + 2 * s:2 * c + 3 * s])

    cb, lg, lb = cb_ref[...], lg_ref[...], lb_ref[...]
    for r0, rows in chunks:
        acc = _conv_chunk(uh_ref[r0:r0 + rows + HIST_PAD, :], cwb_ref, rows) + cb
        mu = jnp.mean(acc, axis=-1, keepdims=True)
        dlt = acc - mu
        var = jnp.mean(dlt * dlt, axis=-1, keepdims=True)
        y = dlt * lax.rsqrt(var + EPS) * lg + lb
        a_ref[0, r0:r0 + rows, :] = (y * jax.nn.sigmoid(y)).astype(_BF16)

    last = uh_ref[tile:tile + HIST_PAD, :]
    nc_ref[0] = last
    uh_ref[0:HIST_PAD, :] = last


def _front(x, meta, g_mix, w_in, hist_padded, conv_w, conv_b, ln_g, ln_b, *, n_tiles, chunk_rows):
    b, t_x, d = x.shape
    t_all = t_x + (N_META if meta is not None else 0)
    tile = t_all // n_tiles
    assert tile * n_tiles == t_all and tile % 16 == 0 and tile >= HIST_PAD
    chunks = tuple((r0, min(chunk_rows, tile - r0)) for r0 in range(0, tile, chunk_rows))
    assert all(rows % 16 == 0 for _, rows in chunks)
    in_w = w_in.shape[1]
    const = lambda shape: pl.BlockSpec(shape, lambda i, j: (0,) * len(shape))
    in_specs = [pl.BlockSpec((1, t_x, d), lambda i, j: (i, 0, 0))]
    args = [x]
    if meta is not None:
        in_specs.append(const((N_META, d)))
        args.append(meta)
    in_specs += [const((1, d)), const((d, in_w)), pl.BlockSpec((1, HIST_PAD, CONV_CH), lambda i, j: (i, 0, 0)),
                 const((CONV_WIDTH, CONV_CH)), const((1, CONV_CH)), const((1, CONV_CH)), const((1, CONV_CH))]
    args += [g_mix, w_in, hist_padded, conv_w, conv_b, ln_g, ln_b]
    row_spec = pl.BlockSpec((1, tile, CONV_CH), lambda i, j: (i, j, 0))
    return pl.pallas_call(
        functools.partial(_front_kernel, tile=tile, has_meta=meta is not None, chunks=chunks),
        grid=(b, n_tiles),
        in_specs=in_specs,
        out_specs=[row_spec] * 4 + [pl.BlockSpec((1, HIST_PAD, CONV_CH), lambda i, j: (i, 0, 0))],
        out_shape=[jax.ShapeDtypeStruct((b, t_all, CONV_CH), _BF16),
                   jax.ShapeDtypeStruct((b, t_all, SB_WIDTH), _BF16),
                   jax.ShapeDtypeStruct((b, t_all, SB_WIDTH), _F32),
                   jax.ShapeDtypeStruct((b, t_all, SB_WIDTH), _F32),
                   jax.ShapeDtypeStruct((b, HIST_PAD, CONV_CH), _F32)],
        scratch_shapes=[pltpu.VMEM((tile, d), _BF16),
                        pltpu.VMEM((HIST_PAD + tile, CONV_CH), _F32),
                        pltpu.VMEM((8 * CONV_WIDTH, CONV_CH), _F32)],
        compiler_params=pltpu.CompilerParams(
            dimension_semantics=("arbitrary", "arbitrary"), vmem_limit_bytes=VMEM_LIMIT),
        name=f"front_t{t_all}",
    )(*args)


def _sb_kernel(*refs, mq, g_lock, n_super, q_off, front_pad, n_past):
    if n_past:
        q_ref, kn_ref, vn_ref, kp_ref, vp_ref, g_ref, tri_ref, o_ref = refs[:8]
    else:
        q_ref, kn_ref, vn_ref, g_ref, tri_ref, o_ref = refs[:6]
    kb_ref, v0_ref, v1_ref, acc_ref, carry_ref = refs[-5:]
    t_new = kn_ref.shape[1]
    rows = kb_ref.shape[0]
    lane_row = lax.broadcasted_iota(jnp.int32, (1, LANES), 1) < SB_HEAD_DIM

    def fill(r0, n, k, v):
        kb_ref[r0:r0 + n, :] = k.astype(_BF16)
        v0_ref[r0:r0 + n, :] = jnp.where(lane_row, v, 0.0).astype(_BF16)
        v1_ref[r0:r0 + n, :] = jnp.where(lane_row, 0.0, v).astype(_BF16)

    new0 = front_pad + n_past
    if front_pad:
        fill(0, front_pad, jnp.zeros((front_pad, LANES), _F32), jnp.zeros((front_pad, LANES), _F32))
    if n_past:
        fill(front_pad, n_past, kp_ref[0], vp_ref[0])
    fill(new0, t_new, kn_ref[0], vn_ref[0])
    tail = rows - new0 - t_new
    if tail:
        fill(new0 + t_new, tail, jnp.zeros((tail, LANES), _F32), jnp.zeros((tail, LANES), _F32))
    if q_off:
        o_ref[0, 0:q_off, :] = jnp.zeros((q_off, LANES), _BF16)

    tri = tri_ref[...]
    gsb = g_ref[...]
    lane_lo = lax.broadcasted_iota(jnp.int32, (mq, LANES), 1) < SB_HEAD_DIM
    col_minus_row = (lax.broadcasted_iota(jnp.int32, (2 * mq, KEY_BLOCK), 1) + KEY_BLOCK
                     - lax.broadcasted_iota(jnp.int32, (2 * mq, KEY_BLOCK), 0) % mq)
    assert (new0 + q_off) % KEY_BLOCK == 0 and front_pad >= KEY_BLOCK
    assert mq == KEY_BLOCK or g_lock * n_super == 1
    end0 = new0 + q_off + KEY_BLOCK
    nt_dims = (((1,), (1,)), ((), ()))

    def superblock(sb, unused):
        i0 = sb * g_lock
        q0s = [pl.multiple_of(q_off + (i0 + g) * mq, 16) for g in range(g_lock)]
        qhs = []
        for g in range(g_lock):
            qb = q_ref[0, pl.ds(q0s[g], mq), :]
            zero_q = jnp.zeros_like(qb)
            qhs.append(jnp.concatenate([jnp.where(lane_lo, qb, zero_q), jnp.where(lane_lo, zero_q, qb)],
                                       axis=0))
        acc_ref[...] = jnp.zeros_like(acc_ref)
        carry_ref[...] = jnp.zeros_like(carry_ref)

        def sweep(state):
            n = state[0]
            low = jnp.full((1, LANES), jnp.inf, _F32)
            starts = [end0 + (i0 + g) * mq - (n + 1) * KEY_CHUNK for g in range(g_lock)]
            r0s = [pl.multiple_of(jnp.maximum(st, 0), KEY_BLOCK) for st in starts]
            zs, cts = [], []
            for g in range(g_lock):
                z = lax.dot_general(qhs[g], kb_ref[pl.ds(r0s[g], KEY_CHUNK), :], nt_dims,
                                    preferred_element_type=_F32)
                hidden_r = jnp.where(col_minus_row < KEY_BLOCK + n * KEY_CHUNK, z[:, KEY_BLOCK:], MASKED_Z)
                zs.append(jnp.concatenate([z[:, :KEY_BLOCK], hidden_r], axis=1))
            for g in range(g_lock):
                z = zs[g]
                s = jnp.maximum(z, 0.0) + jnp.log2(1.0 + jnp.exp2(-jnp.abs(z)))
                hi = s.astype(_BF16)
                lo = (s - hi.astype(_F32)).astype(_BF16)
                cts.append(_dot(jnp.concatenate(
                    [jnp.concatenate([hi[:, KEY_BLOCK:], lo[:, KEY_BLOCK:]], axis=1),
                     jnp.concatenate([hi[:, :KEY_BLOCK], lo[:, :KEY_BLOCK]], axis=1)], axis=0), tri))
            for g in range(g_lock):
                ct_r, ct_l = cts[g][:2 * mq], cts[g][2 * mq:]
                carry = carry_ref[g]
                c_r = ct_r[:, :KEY_BLOCK] + carry
                carry = carry + ct_r[:, KEY_BLOCK:]
                c_l = ct_l[:, :KEY_BLOCK] + carry
                carry = carry + ct_l[:, KEY_BLOCK:]
                carry_ref[g] = carry
                w = jnp.exp2(zs[g] - jnp.concatenate([c_l, c_r], axis=1)).astype(_BF16)
                pv = (_dot(w[:mq], v0_ref[pl.ds(r0s[g], KEY_CHUNK), :])
                      + _dot(w[mq:], v1_ref[pl.ds(r0s[g], KEY_CHUNK), :]))
                acc_ref[g] += jnp.where(starts[g] >= 0, pv, 0.0)
                done = jnp.where(starts[g] - KEY_CHUNK >= 0, 0.0, 2 * EXIT_LOG2)
                low = jnp.minimum(low, jnp.min(carry, axis=0, keepdims=True) + done)
            return n + 1, jnp.where(jnp.min(low) > EXIT_LOG2, 0, 1)

        lax.while_loop(lambda st: st[1] > 0, sweep, (jnp.int32(0), jnp.int32(1)))

        for g in range(g_lock):
            o = acc_ref[g]
            o2 = o * o
            s_all = jnp.sum(o2, axis=-1, keepdims=True)
            s_lo = jnp.sum(jnp.where(lane_lo, o2, 0.0), axis=-1, keepdims=True)
            inv = jnp.where(lane_lo, lax.rsqrt(s_lo / SB_HEAD_DIM + EPS),
                            lax.rsqrt((s_all - s_lo) / SB_HEAD_DIM + EPS))
            o_ref[0, pl.ds(q0s[g], mq), :] = (o * inv * gsb).astype(_BF16)
        return unused

    lax.fori_loop(0, n_super, superblock, 0)


def _tri_matrix():
    r = lax.broadcasted_iota(jnp.int32, (2 * KEY_BLOCK, 2 * KEY_BLOCK), 0) % KEY_BLOCK
    c = lax.broadcasted_iota(jnp.int32, (2 * KEY_BLOCK, 2 * KEY_BLOCK), 1)
    return jnp.where((c >= KEY_BLOCK) | (r >= c), 1.0, 0.0).astype(_BF16)


def _stick_breaking(q, k_new, v_new, k_past, v_past, g_sb, *, mq, g_lock, q_off, front_pad):
    b, t_new, _ = k_new.shape
    n_past = 0 if k_past is None else k_past.shape[1]
    n_super = (t_new - q_off) // (mq * g_lock)
    assert n_super * mq * g_lock == t_new - q_off
    last_end = front_pad + n_past + q_off + KEY_BLOCK + (n_super * g_lock - 1) * mq
    rows = max(pl.cdiv(front_pad + n_past + t_new, KEY_BLOCK) * KEY_BLOCK, last_end)
    pairs = SB_WIDTH // LANES
    new_spec = pl.BlockSpec((1, t_new, LANES), lambda i, p: (i, 0, p))
    in_specs = [new_spec, new_spec, new_spec]
    args = [q, k_new, v_new]
    if n_past:
        past_spec = pl.BlockSpec((1, n_past, LANES), lambda i, p: (i, 0, p))
        in_specs += [past_spec, past_spec]
        args += [k_past, v_past]
    in_specs += [pl.BlockSpec((1, LANES), lambda i, p: (0, p)),
                 pl.BlockSpec((2 * KEY_BLOCK, 2 * KEY_BLOCK), lambda i, p: (0, 0))]
    args += [g_sb, _tri_matrix()]
    return pl.pallas_call(
        functools.partial(_sb_kernel, mq=mq, g_lock=g_lock, n_super=n_super, q_off=q_off,
                          front_pad=front_pad, n_past=n_past),
        grid=(b, pairs),
        in_specs=in_specs,
        out_specs=pl.BlockSpec((1, t_new, LANES), lambda i, p: (i, 0, p)),
        out_shape=jax.ShapeDtypeStruct((b, t_new, SB_WIDTH), _BF16),
        scratch_shapes=[pltpu.VMEM((rows, LANES), _BF16)] * 3
        + [pltpu.VMEM((g_lock, mq, LANES), _F32), pltpu.VMEM((g_lock, 2 * mq, LANES), _F32)],
        compiler_params=pltpu.CompilerParams(
            dimension_semantics=("arbitrary", "arbitrary"), vmem_limit_bytes=VMEM_LIMIT),
        name=f"stick_breaking_t{t_new}",
    )(*args)


def _split_bf16(x):
    hi = x.astype(_BF16)
    return hi, (x - hi.astype(_F32)).astype(_BF16)


def _route(logits):
    n = logits.shape[1]
    row = lax.broadcasted_iota(jnp.int32, (ROUTER_ROWS, n), 0)
    neg = -jnp.inf
    big = 4 * LANES
    is_g = (row >= N_EXPERTS) & (row < N_EXPERTS + N_GROUPS)
    lg = jnp.where(is_g, logits, neg)
    g_max = jnp.max(lg, axis=0, keepdims=True)
    p_g_sel = 1.0 / jnp.sum(jnp.where(is_g, jnp.exp(lg - g_max), 0.0), axis=0, keepdims=True)
    g_sel = jnp.min(jnp.where(lg == g_max, row, big), axis=0, keepdims=True) - N_EXPERTS
    in_grp = (row < N_EXPERTS) & ((row // EXPERTS_PER_GROUP) == g_sel)
    le = jnp.where(in_grp, logits, neg)
    e_max = jnp.max(le, axis=0, keepdims=True)
    pe = jnp.where(in_grp, jnp.exp(le - e_max), 0.0)
    pe = pe / jnp.sum(pe, axis=0, keepdims=True)
    pe = jnp.where(in_grp, pe, -1.0)
    top1 = jnp.max(pe, axis=0, keepdims=True)
    i1 = jnp.min(jnp.where(pe == top1, row, big), axis=0, keepdims=True)
    pe_rest = jnp.where(row == i1, -1.0, pe)
    top2 = jnp.max(pe_rest, axis=0, keepdims=True)
    i2 = jnp.min(jnp.where(pe_rest == top2, row, big), axis=0, keepdims=True)
    denom = top1 + top2
    gates = jnp.where(row == i1, top1 / denom, jnp.where(row == i2, top2 / denom, 0.0)) * p_g_sel
    return jnp.where(row == GROUP_ROW, g_sel.astype(_F32), gates)


def _tail_kernel(a_ref, b_ref, x_ref, wa_ref, wb_ref, gffn_ref, wrh_ref, wrl_ref, br_ref,
                 win_ref, wout_ref, gf_ref, y_ref,
                 h_ref, lg_ref, hs_ref, gs_ref, ys_ref, ut_ref, pos_ref, meta_ref,
                 *, tm, tms_p, ffn_sizes, off, tiles_per_batch):
    i = pl.program_id(0)
    s = pl.program_id(1)
    steps_per_group = EXPERTS_PER_GROUP // EXPERT_STEP
    nt = (((1,), (1,)), ((), ()))
    tms = hs_ref.shape[0]

    @pl.when((i == 0) & (s == 0))
    def _():
        ut_ref[...] = jnp.where(lax.broadcasted_iota(jnp.int32, (tm, tm), 0)
                                < lax.broadcasted_iota(jnp.int32, (tm, tm), 1), 1.0, 0.0).astype(_BF16)

    @pl.when(s == 0)
    def _():
        a0 = off + (i % tiles_per_batch) * tm
        wrh = wrh_ref[...]
        for c in range(tm // PERM_ROWS):
            rows = pl.ds(c * PERM_ROWS, PERM_ROWS)
            arows = pl.ds(pl.multiple_of(a0 + c * PERM_ROWS, 16), PERM_ROWS)
            x2 = (x_ref[rows, :] + _dot(a_ref[0, arows, :], wa_ref[...])
                  + _dot(b_ref[0, arows, :], wb_ref[...]))
            y_ref[rows, :] = x2
            h_hi, h_lo = _split_bf16(_rms(x2, gffn_ref[...]))
            h_ref[rows, :] = h_hi
            lg_ref[:, c * PERM_ROWS:(c + 1) * PERM_ROWS] = (
                lax.dot_general(wrh, h_hi, nt, preferred_element_type=_F32)
                + lax.dot_general(wrh, h_lo, nt, preferred_element_type=_F32)
                + lax.dot_general(wrl_ref[...], h_hi, nt, preferred_element_type=_F32))
        gates = _route(lg_ref[0:ROUTER_ROWS, :] + jnp.tile(br_ref[...], (1, tm // LANES))[:ROUTER_ROWS])
        gt = jnp.concatenate([gates, jnp.zeros((LANES - ROUTER_ROWS, tm), _F32)], axis=0)

        gid = gt[GROUP_ROW:GROUP_ROW + 1, :].astype(jnp.int32)
        grow = lax.broadcasted_iota(jnp.int32, (8, tm), 0)
        onehot = jnp.where(grow == gid, 1.0, 0.0)
        rank = _dot(onehot.astype(_BF16), ut_ref[...])
        goff = jnp.int32(0)
        off_rows = jnp.zeros((8, tm), _F32)
        for g in range(N_GROUPS):
            cnt = jnp.sum(jnp.where(grow == g, onehot, 0.0)).astype(jnp.int32)
            meta_ref[g] = goff
            meta_ref[N_GROUPS + g] = cnt
            off_rows = jnp.where(grow == g, goff.astype(_F32), off_rows)
            goff = goff + (cnt + GROUP_ALIGN - 1) // GROUP_ALIGN * GROUP_ALIGN
        pos = jnp.sum(onehot * (rank + off_rows), axis=0, keepdims=True)
        pos_ref[...] = jnp.transpose(jnp.broadcast_to(pos, (LANES, tm)))
        pos_i = pos.astype(jnp.int32)
        g_hi, g_lo = _split_bf16(gt)
        g_hl = jnp.concatenate([g_hi, g_lo], axis=0)
        h = h_ref[...]
        for c in range(tms_p // PERM_ROWS):
            prow = lax.broadcasted_iota(jnp.int32, (PERM_ROWS, tm), 0) + c * PERM_ROWS
            perm = jnp.where(prow == pos_i, 1.0, 0.0).astype(_BF16)
            rows = pl.ds(c * PERM_ROWS, PERM_ROWS)
            hs_ref[rows, :] = _dot(perm, h).astype(_BF16)
            g2 = lax.dot_general(perm, g_hl, nt, preferred_element_type=_F32)
            gs_ref[rows, :] = g2[:, :LANES] + g2[:, LANES:]
        hs_ref[tms_p:, :] = jnp.zeros((tms - tms_p, D_MODEL), _BF16)
        gs_ref[tms_p:, :] = jnp.zeros((tms - tms_p, LANES), _F32)
        ys_ref[...] = jnp.zeros_like(ys_ref)

    grp = s // steps_per_group
    goff = meta_ref[grp]
    cnt = meta_ref[N_GROUPS + grp]

    def ffn(r0, m):
        rows = pl.ds(pl.multiple_of(r0, GROUP_ALIGN), m)
        hb = hs_ref[rows, :]
        gsb = gs_ref[rows, :]
        lane = lax.broadcasted_iota(jnp.int32, (m, LANES), 1)
        acc = jnp.zeros((m, D_MODEL), _F32)
        for j in range(EXPERT_STEP):
            gu = _dot(hb, win_ref[j])
            act = (jax.nn.silu(gu[:, :D_EXPERT]) * gu[:, D_EXPERT:]).astype(_BF16)
            gate = jnp.sum(jnp.where(lane == s * EXPERT_STEP + j, gsb, 0.0), axis=-1, keepdims=True)
            acc = acc + gate * _dot(act, wout_ref[j])
        ys_ref[rows, :] += acc

    for idx, m in enumerate(ffn_sizes):
        lo = ffn_sizes[idx - 1] if idx else 0

        @pl.when((cnt > lo) & (cnt <= m))
        def _(m=m):
            ffn(goff, m)

    @pl.when(cnt > ffn_sizes[-1])
    def _():
        m = ffn_sizes[len(ffn_sizes) // 2]

        def block(b, carry):
            ffn(goff + b * m, m)
            return carry

        lax.fori_loop(0, (cnt + m - 1) // m, block, 0)

    @pl.when(s == pl.num_programs(1) - 1)
    def _():
        hs_ref[0:tms_p, :] = ys_ref[0:tms_p, :].astype(_BF16)
        for c in range(tm // PERM_ROWS):
            rows = pl.ds(c * PERM_ROWS, PERM_ROWS)
            pos_c = jnp.tile(pos_ref[rows, :], (1, tms_p // LANES)).astype(jnp.int32)
            unperm = jnp.where(lax.broadcasted_iota(jnp.int32, (PERM_ROWS, tms_p), 1) == pos_c,
                               1.0, 0.0).astype(_BF16)
            y_ref[rows, :] = _rms(y_ref[rows, :] + _dot(unperm, hs_ref[0:tms_p, :]), gf_ref[...])


def _tail(a, b_out, x, wa, wb, g_ffn, wr_hi, wr_lo, br, w_e_in, w_e_out, g_final, *, tm, off):
    bsz, t_a, _ = a.shape
    rows, d = x.shape
    t_x = rows // bsz
    assert t_x % tm == 0 and tm % PERM_ROWS == 0
    tiles_per_batch = t_x // tm
    tms_p = pl.cdiv(tm + N_GROUPS * (GROUP_ALIGN - 1), PERM_ROWS) * PERM_ROWS
    unit = max(GROUP_ALIGN, tm // (8 * N_GROUPS))
    ffn_sizes = tuple(unit * f for f in range(7, 13))
    tms = tms_p + ffn_sizes[-1]
    n_steps = N_EXPERTS // EXPERT_STEP
    res = pl.BlockSpec((1, t_a, CONV_CH), lambda i, s: (i // tiles_per_batch, 0, 0))
    tok = pl.BlockSpec((tm, d), lambda i, s: (i, 0))
    const = lambda shape: pl.BlockSpec(shape, lambda i, s: (0,) * len(shape), pipeline_mode=pl.Buffered(1))
    return pl.pallas_call(
        functools.partial(_tail_kernel, tm=tm, tms_p=tms_p, ffn_sizes=ffn_sizes, off=off,
                          tiles_per_batch=tiles_per_batch),
        grid=(rows // tm, n_steps),
        in_specs=[res, res, tok, const(wa.shape), const(wb.shape), const((1, d)),
                  const(wr_hi.shape), const(wr_lo.shape), const((LANES, LANES)),
                  pl.BlockSpec((EXPERT_STEP, d, 2 * D_EXPERT), lambda i, s: (s, 0, 0)),
                  pl.BlockSpec((EXPERT_STEP, D_EXPERT, d), lambda i, s: (s, 0, 0)),
                  const((1, d))],
        out_specs=tok,
        out_shape=jax.ShapeDtypeStruct((rows, d), _F32),
        scratch_shapes=[pltpu.VMEM((tm, d), _BF16),
                        pltpu.VMEM((LANES, tm), _F32),
                        pltpu.VMEM((tms, d), _BF16),
                        pltpu.VMEM((tms, LANES), _F32),
                        pltpu.VMEM((tms, d), _F32),
                        pltpu.VMEM((tm, tm), _BF16),
                        pltpu.VMEM((tm, LANES), _F32),
                        pltpu.SMEM((2 * N_GROUPS,), jnp.int32)],
        compiler_params=pltpu.CompilerParams(
            dimension_semantics=("arbitrary", "arbitrary"), vmem_limit_bytes=MOE_VMEM_LIMIT),
        name=f"tail_r{rows}",
    )(a, b_out, x, wa, wb, g_ffn, wr_hi, wr_lo, br, w_e_in, w_e_out, g_final)


def kernel(x_prompt, x_sample, cache_k, cache_v, cache_conv, meta_tokens, g_mix, w_in, conv_w, conv_b,
           conv_ln_g, conv_ln_b, g_sb, w_out, g_ffn, w_router_group, b_router_group, w_router_expert,
           b_router_expert, w_expert_in, w_expert_out, g_final):
    depth = w_in.shape[0]
    assert depth == 1, "single-layer trunk"
    bp, tp, d = x_prompt.shape
    bs, ts, _ = x_sample.shape
    n_past = cache_k.shape[2]

    w_in_b = w_in[0].astype(_BF16)
    wa = w_out[0, :CONV_CH].astype(_BF16)
    wb = w_out[0, CONV_CH:].astype(_BF16)
    w_e_in = w_expert_in[0].astype(_BF16)
    w_e_out = w_expert_out[0].astype(_BF16)
    pad = LANES - N_EXPERTS - N_GROUPS
    wr = jnp.concatenate([w_router_expert[0].T, w_router_group[0].T, jnp.zeros((pad, d), _F32)], axis=0)
    wr_hi = wr.astype(_BF16)
    wr_lo = (wr - wr_hi.astype(_F32)).astype(_BF16)
    br = jnp.concatenate([b_router_expert[0], b_router_group[0], jnp.zeros((pad,), _F32)])
    br = jnp.broadcast_to(br[:, None], (LANES, LANES))
    g_mix2, g_ffn2, g_fin2, g_sb2 = g_mix[0][None], g_ffn[0][None], g_final[None], g_sb[0][None]
    cb, lg, lb = conv_b[0][None], conv_ln_g[0][None], conv_ln_b[0][None]

    def tail(a, b_out, x, *, off, tm):
        rows = x.shape[0] * x.shape[1]
        return _tail(a, b_out, x.reshape(rows, d), wa, wb, g_ffn2, wr_hi, wr_lo, br, w_e_in, w_e_out, g_fin2,
                     tm=tm, off=off).reshape(x.shape)

    a_p, q_p, k_p, v_p, nc_p = _front(x_prompt, meta_tokens, g_mix2, w_in_b,
                                      jnp.zeros((bp, HIST_PAD, CONV_CH), _F32), conv_w[0], cb, lg, lb,
                                      n_tiles=3, chunk_rows=256)
    b_p = _stick_breaking(q_p, k_p, v_p, None, None, g_sb2, mq=KEY_BLOCK, g_lock=8, q_off=N_META,
                          front_pad=2 * KEY_BLOCK - N_META)
    y_p = tail(a_p, b_p, x_prompt, off=N_META, tm=1024)

    hist = jnp.pad(cache_conv[0], ((0, 0), (HIST_PAD - CONV_HIST, 0), (0, 0)))
    a_s, q_s, k_s, v_s, nc_s = _front(x_sample, None, g_mix2, w_in_b, hist, conv_w[0], cb, lg, lb,
                                      n_tiles=1, chunk_rows=ts)
    b_s = _stick_breaking(q_s, k_s, v_s, cache_k[0].reshape(bs, n_past, SB_WIDTH),
                          cache_v[0].reshape(bs, n_past, SB_WIDTH), g_sb2, mq=ts, g_lock=1, q_off=0,
                          front_pad=KEY_BLOCK)
    flat = lambda t: t.reshape(1, bs * ts, t.shape[-1])
    y_s = tail(flat(a_s), flat(b_s), flat(x_sample), off=0, tm=bs * ts).reshape(x_sample.shape)

    heads = (SB_HEADS, SB_HEAD_DIM)
    lead = HIST_PAD - CONV_HIST
    return (y_p, y_s,
            k_p.reshape(1, bp, tp + N_META, *heads), v_p.reshape(1, bp, tp + N_META, *heads),
            nc_p[None, :, lead:],
            k_s.reshape(1, bs, ts, *heads), v_s.reshape(1, bs, ts, *heads),
            nc_s[None, :, lead:])
```

```python
import functools

import jax
import jax.numpy as jnp
from jax import lax
from jax.experimental import pallas as pl
from jax.experimental.pallas import tpu as pltpu

D_MODEL = 1024
N_META = 16
CONV_CH = 512
CONV_WIDTH = 31
CONV_HIST = CONV_WIDTH - 1
SB_HEADS = 8
SB_HEAD_DIM = 64
SB_WIDTH = SB_HEADS * SB_HEAD_DIM
N_GROUPS = 4
EXPERTS_PER_GROUP = 8
N_EXPERTS = N_GROUPS * EXPERTS_PER_GROUP
D_EXPERT = 256
EPS = 1e-6

LANES = 128
HIST_PAD = 32
KEY_BLOCK = 128
KEY_CHUNK = 256
LOG2_E = 1.4426950408889634
MASKED_Z = -1e30
EXIT_LOG2 = 128.0
ROUTER_ROWS = 40
GROUP_ROW = N_EXPERTS
GROUP_ALIGN = 16
PERM_ROWS = 256
EXPERT_STEP = 4
VMEM_LIMIT = 52 * 1024 * 1024
MOE_VMEM_LIMIT = 58 * 1024 * 1024

_F32 = jnp.float32
_BF16 = jnp.bfloat16


def _dot(a, b):
    return jnp.dot(a, b, preferred_element_type=_F32)


def _rms(x, g):
    return x * lax.rsqrt(jnp.mean(x * x, axis=-1, keepdims=True) + EPS) * g


def _conv_chunk(win, cwb_ref, rows):
    lead = HIST_PAD - CONV_HIST
    tiles = rows // 8
    acc = jnp.zeros((tiles, 8, CONV_CH), _F32)
    for s in range(8):
        taps = [k for k in range(CONV_WIDTH) if (lead + k) % 8 == s]
        shifted = win if s == 0 else win[s:s + rows + HIST_PAD - 8, :]
        for k in taps:
            off = lead + k - s
            tap = shifted[off:off + rows, :].reshape(tiles, 8, CONV_CH)
            acc = acc + cwb_ref[8 * k:8 * k + 8, :][None] * tap
    return acc.reshape(rows, CONV_CH)


def _front_kernel(*refs, tile, has_meta, chunks):
    if has_meta:
        x_ref, meta_ref = refs[:2]
        refs = refs[2:]
    else:
        x_ref = refs[0]
        refs = refs[1:]
    (g_ref, w_ref, hist_ref, cw_ref, cb_ref, lg_ref, lb_ref,
     a_ref, q_ref, k_ref, v_ref, nc_ref, h_ref, uh_ref, cwb_ref) = refs
    t = pl.program_id(1)
    g = g_ref[...]
    if has_meta:
        main0 = pl.multiple_of(t * tile, 8)
        h_ref[N_META:, :] = _rms(x_ref[0, pl.ds(main0, tile - N_META), :], g).astype(_BF16)
        head0 = pl.multiple_of(jnp.maximum(t * tile - N_META, 0), 8)
        xh = jnp.where(t == 0, meta_ref[...], x_ref[0, pl.ds(head0, N_META), :])
        h_ref[0:N_META, :] = _rms(xh, g).astype(_BF16)
    else:
        h_ref[...] = _rms(x_ref[0], g).astype(_BF16)

    @pl.when(t == 0)
    def _():
        uh_ref[0:HIST_PAD, :] = hist_ref[0]

    for k in range(CONV_WIDTH):
        cwb_ref[8 * k:8 * k + 8, :] = jnp.broadcast_to(cw_ref[k:k + 1, :], (8, CONV_CH))

    h = h_ref[...]
    c, s = CONV_CH, SB_WIDTH
    uh_ref[HIST_PAD:, :] = _dot(h, w_ref[:, 0:c]) * jax.nn.sigmoid(_dot(h, w_ref[:, c:2 * c]))
    q_ref[0] = (_dot(h, w_ref[:, 2 * c:2 * c + s]) * (SB_HEAD_DIM ** -0.5 * LOG2_E)).astype(_BF16)
    k_ref[0] = _dot(h, w_ref[:, 2 * c + s:2 * c + 2 * s])
    v_ref[0] = _dot(h, w_ref[:, 2 * c + 2 * s:2 * c + 3 * s])

    cb, lg, lb = cb_ref[...], lg_ref[...], lb_ref[...]
    for r0, rows in chunks:
        acc = _conv_chunk(uh_ref[r0:r0 + rows + HIST_PAD, :], cwb_ref, rows) + cb
        mu = jnp.mean(acc, axis=-1, keepdims=True)
        dlt = acc - mu
        var = jnp.mean(dlt * dlt, axis=-1, keepdims=True)
        y = dlt * lax.rsqrt(var + EPS) * lg + lb
        a_ref[0, r0:r0 + rows, :] = (y * jax.nn.sigmoid(y)).astype(_BF16)

    last = uh_ref[tile:tile + HIST_PAD, :]
    nc_ref[0] = last
    uh_ref[0:HIST_PAD, :] = last


def _front(x, meta, g_mix, w_in, hist_padded, conv_w, conv_b, ln_g, ln_b, *, n_tiles, chunk_rows):
    b, t_x, d = x.shape
    t_all = t_x + (N_META if meta is not None else 0)
    tile = t_all // n_tiles
    assert tile * n_tiles == t_all and tile % 16 == 0 and tile >= HIST_PAD
    chunks = tuple((r0, min(chunk_rows, tile - r0)) for r0 in range(0, tile, chunk_rows))
    assert all(rows % 16 == 0 for _, rows in chunks)
    in_w = w_in.shape[1]
    const = lambda shape: pl.BlockSpec(shape, lambda i, j: (0,) * len(shape))
    in_specs = [pl.BlockSpec((1, t_x, d), lambda i, j: (i, 0, 0))]
    args = [x]
    if meta is not None:
        in_specs.append(const((N_META, d)))
        args.append(meta)
    in_specs += [const((1, d)), const((d, in_w)), pl.BlockSpec((1, HIST_PAD, CONV_CH), lambda i, j: (i, 0, 0)),
                 const((CONV_WIDTH, CONV_CH)), const((1, CONV_CH)), const((1, CONV_CH)), const((1, CONV_CH))]
    args += [g_mix, w_in, hist_padded, conv_w, conv_b, ln_g, ln_b]
    row_spec = pl.BlockSpec((1, tile, CONV_CH), lambda i, j: (i, j, 0))
    return pl.pallas_call(
        functools.partial(_front_kernel, tile=tile, has_meta=meta is not None, chunks=chunks),
        grid=(b, n_tiles),
        in_specs=in_specs,
        out_specs=[row_spec] * 4 + [pl.BlockSpec((1, HIST_PAD, CONV_CH), lambda i, j: (i, 0, 0))],
        out_shape=[jax.ShapeDtypeStruct((b, t_all, CONV_CH), _BF16),
                   jax.ShapeDtypeStruct((b, t_all, SB_WIDTH), _BF16),
                   jax.ShapeDtypeStruct((b, t_all, SB_WIDTH), _F32),
                   jax.ShapeDtypeStruct((b, t_all, SB_WIDTH), _F32),
                   jax.ShapeDtypeStruct((b, HIST_PAD, CONV_CH), _F32)],
        scratch_shapes=[pltpu.VMEM((tile, d), _BF16),
                        pltpu.VMEM((HIST_PAD + tile, CONV_CH), _F32),
                        pltpu.VMEM((8 * CONV_WIDTH, CONV_CH), _F32)],
        compiler_params=pltpu.CompilerParams(
            dimension_semantics=("arbitrary", "arbitrary"), vmem_limit_bytes=VMEM_LIMIT),
        name=f"front_t{t_all}",
    )(*args)


def _sb_kernel(*refs, mq, g_lock, n_super, q_off, front_pad, n_past):
    if n_past:
        q_ref, kn_ref, vn_ref, kp_ref, vp_ref, g_ref, tri_ref, o_ref = refs[:8]
    else:
        q_ref, kn_ref, vn_ref, g_ref, tri_ref, o_ref = refs[:6]
    kb_ref, v0_ref, v1_ref, acc_ref, carry_ref = refs[-5:]
    t_new = kn_ref.shape[1]
    rows = kb_ref.shape[0]
    lane_row = lax.broadcasted_iota(jnp.int32, (1, LANES), 1) < SB_HEAD_DIM

    def fill(r0, n, k, v):
        kb_ref[r0:r0 + n, :] = k.astype(_BF16)
        v0_ref[r0:r0 + n, :] = jnp.where(lane_row, v, 0.0).astype(_BF16)
        v1_ref[r0:r0 + n, :] = jnp.where(lane_row, 0.0, v).astype(_BF16)

    new0 = front_pad + n_past
    if front_pad:
        fill(0, front_pad, jnp.zeros((front_pad, LANES), _F32), jnp.zeros((front_pad, LANES), _F32))
    if n_past:
        fill(front_pad, n_past, kp_ref[0], vp_ref[0])
    fill(new0, t_new, kn_ref[0], vn_ref[0])
    tail = rows - new0 - t_new
    if tail:
        fill(new0 + t_new, tail, jnp.zeros((tail, LANES), _F32), jnp.zeros((tail, LANES), _F32))
    if q_off:
        o_ref[0, 0:q_off, :] = jnp.zeros((q_off, LANES), _BF16)

    tri = tri_ref[...]
    gsb = g_ref[...]
    lane_lo = lax.broadcasted_iota(jnp.int32, (mq, LANES), 1) < SB_HEAD_DIM
    col_minus_row = (lax.broadcasted_iota(jnp.int32, (2 * mq, KEY_BLOCK), 1) + KEY_BLOCK
                     - lax.broadcasted_iota(jnp.int32, (2 * mq, KEY_BLOCK), 0) % mq)
    assert (new0 + q_off) % KEY_BLOCK == 0 and front_pad >= KEY_BLOCK
    assert mq == KEY_BLOCK or g_lock * n_super == 1
    end0 = new0 + q_off + KEY_BLOCK
    nt_dims = (((1,), (1,)), ((), ()))

    def superblock(sb, unused):
        i0 = sb * g_lock
        q0s = [pl.multiple_of(q_off + (i0 + g) * mq, 16) for g in range(g_lock)]
        qhs = []
        for g in range(g_lock):
            qb = q_ref[0, pl.ds(q0s[g], mq), :]
            zero_q = jnp.zeros_like(qb)
            qhs.append(jnp.concatenate([jnp.where(lane_lo, qb, zero_q), jnp.where(lane_lo, zero_q, qb)],
                                       axis=0))
        acc_ref[...] = jnp.zeros_like(acc_ref)
        carry_ref[...] = jnp.zeros_like(carry_ref)

        def sweep(state):
            n = state[0]
            low = jnp.full((1, LANES), jnp.inf, _F32)
            starts = [end0 + (i0 + g) * mq - (n + 1) * KEY_CHUNK for g in range(g_lock)]
            r0s = [pl.multiple_of(jnp.maximum(st, 0), KEY_BLOCK) for st in starts]
            zs, cts = [], []
            for g in range(g_lock):
                z = lax.dot_general(qhs[g], kb_ref[pl.ds(r0s[g], KEY_CHUNK), :], nt_dims,
                                    preferred_element_type=_F32)
                hidden_r = jnp.where(col_minus_row < KEY_BLOCK + n * KEY_CHUNK, z[:, KEY_BLOCK:], MASKED_Z)
                zs.append(jnp.concatenate([z[:, :KEY_BLOCK], hidden_r], axis=1))
            for g in range(g_lock):
                z = zs[g]
                s = jnp.maximum(z, 0.0) + jnp.log2(1.0 + jnp.exp2(-jnp.abs(z)))
                hi = s.astype(_BF16)
                lo = (s - hi.astype(_F32)).astype(_BF16)
                cts.append(_dot(jnp.concatenate(
                    [jnp.concatenate([hi[:, KEY_BLOCK:], lo[:, KEY_BLOCK:]], axis=1),
                     jnp.concatenate([hi[:, :KEY_BLOCK], lo[:, :KEY_BLOCK]], axis=1)], axis=0), tri))
            for g in range(g_lock):
                ct_r, ct_l = cts[g][:2 * mq], cts[g][2 * mq:]
                carry = carry_ref[g]
                c_r = ct_r[:, :KEY_BLOCK] + carry
                carry = carry + ct_r[:, KEY_BLOCK:]
                c_l = ct_l[:, :KEY_BLOCK] + carry
                carry = carry + ct_l[:, KEY_BLOCK:]
                carry_ref[g] = carry
                w = jnp.exp2(zs[g] - jnp.concatenate([c_l, c_r], axis=1)).astype(_BF16)
                pv = (_dot(w[:mq], v0_ref[pl.ds(r0s[g], KEY_CHUNK), :])
                      + _dot(w[mq:], v1_ref[pl.ds(r0s[g], KEY_CHUNK), :]))
                acc_ref[g] += jnp.where(starts[g] >= 0, pv, 0.0)
                done = jnp.where(starts[g] - KEY_CHUNK >= 0, 0.0, 2 * EXIT_LOG2)
                low = jnp.minimum(low, jnp.min(carry, axis=0, keepdims=True) + done)
            return n + 1, jnp.where(jnp.min(low) > EXIT_LOG2, 0, 1)

        lax.while_loop(lambda st: st[1] > 0, sweep, (jnp.int32(0), jnp.int32(1)))

        for g in range(g_lock):
            o = acc_ref[g]
            o2 = o * o
            s_all = jnp.sum(o2, axis=-1, keepdims=True)
            s_lo = jnp.sum(jnp.where(lane_lo, o2, 0.0), axis=-1, keepdims=True)
            inv = jnp.where(lane_lo, lax.rsqrt(s_lo / SB_HEAD_DIM + EPS),
                            lax.rsqrt((s_all - s_lo) / SB_HEAD_DIM + EPS))
            o_ref[0, pl.ds(q0s[g], mq), :] = (o * inv * gsb).astype(_BF16)
        return unused

    lax.fori_loop(0, n_super, superblock, 0)


def _tri_matrix():
    r = lax.broadcasted_iota(jnp.int32, (2 * KEY_BLOCK, 2 * KEY_BLOCK), 0) % KEY_BLOCK
    c = lax.broadcasted_iota(jnp.int32, (2 * KEY_BLOCK, 2 * KEY_BLOCK), 1)
    return jnp.where((c >= KEY_BLOCK) | (r >= c), 1.0, 0.0).astype(_BF16)


def _stick_breaking(q, k_new, v_new, k_past, v_past, g_sb, *, mq, g_lock, q_off, front_pad):
    b, t_new, _ = k_new.shape
    n_past = 0 if k_past is None else k_past.shape[1]
    n_super = (t_new - q_off) // (mq * g_lock)
    assert n_super * mq * g_lock == t_new - q_off
    last_end = front_pad + n_past + q_off + KEY_BLOCK + (n_super * g_lock - 1) * mq
    rows = max(pl.cdiv(front_pad + n_past + t_new, KEY_BLOCK) * KEY_BLOCK, last_end)
    pairs = SB_WIDTH // LANES
    new_spec = pl.BlockSpec((1, t_new, LANES), lambda i, p: (i, 0, p))
    in_specs = [new_spec, new_spec, new_spec]
    args = [q, k_new, v_new]
    if n_past:
        past_spec = pl.BlockSpec((1, n_past, LANES), lambda i, p: (i, 0, p))
        in_specs += [past_spec, past_spec]
        args += [k_past, v_past]
    in_specs += [pl.BlockSpec((1, LANES), lambda i, p: (0, p)),
                 pl.BlockSpec((2 * KEY_BLOCK, 2 * KEY_BLOCK), lambda i, p: (0, 0))]
    args += [g_sb, _tri_matrix()]
    return pl.pallas_call(
        functools.partial(_sb_kernel, mq=mq, g_lock=g_lock, n_super=n_super, q_off=q_off,
                          front_pad=front_pad, n_past=n_past),
        grid=(b, pairs),
        in_specs=in_specs,
        out_specs=pl.BlockSpec((1, t_new, LANES), lambda i, p: (i, 0, p)),
        out_shape=jax.ShapeDtypeStruct((b, t_new, SB_WIDTH), _BF16),
        scratch_shapes=[pltpu.VMEM((rows, LANES), _BF16)] * 3
        + [pltpu.VMEM((g_lock, mq, LANES), _F32), pltpu.VMEM((g_lock, 2 * mq, LANES), _F32)],
        compiler_params=pltpu.CompilerParams(
            dimension_semantics=("arbitrary", "arbitrary"), vmem_limit_bytes=VMEM_LIMIT),
        name=f"stick_breaking_t{t_new}",
    )(*args)


def _split_bf16(x):
    hi = x.astype(_BF16)
    return hi, (x - hi.astype(_F32)).astype(_BF16)


def _route(logits):
    n = logits.shape[1]
    row = lax.broadcasted_iota(jnp.int32, (ROUTER_ROWS, n), 0)
    neg = -jnp.inf
    big = 4 * LANES
    is_g = (row >= N_EXPERTS) & (row < N_EXPERTS + N_GROUPS)
    lg = jnp.where(is_g, logits, neg)
    g_max = jnp.max(lg, axis=0, keepdims=True)
    p_g_sel = 1.0 / jnp.sum(jnp.where(is_g, jnp.exp(lg - g_max), 0.0), axis=0, keepdims=True)
    g_sel = jnp.min(jnp.where(lg == g_max, row, big), axis=0, keepdims=True) - N_EXPERTS
    in_grp = (row < N_EXPERTS) & ((row // EXPERTS_PER_GROUP) == g_sel)
    le = jnp.where(in_grp, logits, neg)
    e_max = jnp.max(le, axis=0, keepdims=True)
    pe = jnp.where(in_grp, jnp.exp(le - e_max), 0.0)
    pe = pe / jnp.sum(pe, axis=0, keepdims=True)
    pe = jnp.where(in_grp, pe, -1.0)
    top1 = jnp.max(pe, axis=0, keepdims=True)
    i1 = jnp.min(jnp.where(pe == top1, row, big), axis=0, keepdims=True)
    pe_rest = jnp.where(row == i1, -1.0, pe)
    top2 = jnp.max(pe_rest, axis=0, keepdims=True)
    i2 = jnp.min(jnp.where(pe_rest == top2, row, big), axis=0, keepdims=True)
    denom = top1 + top2
    gates = jnp.where(row == i1, top1 / denom, jnp.where(row == i2, top2 / denom, 0.0)) * p_g_sel
    return jnp.where(row == GROUP_ROW, g_sel.astype(_F32), gates)


def _tail_kernel(a_ref, b_ref, x_ref, wa_ref, wb_ref, gffn_ref, wrh_ref, wrl_ref, br_ref,
                 win_ref, wout_ref, gf_ref, y_ref,
                 h_ref, lg_ref, hs_ref, gs_ref, ys_ref, ut_ref, pos_ref, meta_ref,
                 *, tm, tms_p, ffn_sizes, off, tiles_per_batch):
    i = pl.program_id(0)
    s = pl.program_id(1)
    steps_per_group = EXPERTS_PER_GROUP // EXPERT_STEP
    nt = (((1,), (1,)), ((), ()))
    tms = hs_ref.shape[0]

    @pl.when((i == 0) & (s == 0))
    def _():
        ut_ref[...] = jnp.where(lax.broadcasted_iota(jnp.int32, (tm, tm), 0)
                                < lax.broadcasted_iota(jnp.int32, (tm, tm), 1), 1.0, 0.0).astype(_BF16)

    @pl.when(s == 0)
    def _():
        a0 = off + (i % tiles_per_batch) * tm
        wrh = wrh_ref[...]
        def mix(c):
            rows = pl.ds(c * PERM_ROWS, PERM_ROWS)
            arows = pl.ds(pl.multiple_of(a0 + c * PERM_ROWS, 16), PERM_ROWS)
            x2 = (x_ref[rows, :] + _dot(a_ref[0, arows, :], wa_ref[...])
                  + _dot(b_ref[0, arows, :], wb_ref[...]))
            y_ref[rows, :] = x2
            h_hi, h_lo = _split_bf16(_rms(x2, gffn_ref[...]))
            h_ref[rows, :] = h_hi
            return h_hi, h_lo

        def logits(c, h_hi, h_lo):
            lg_ref[:, c * PERM_ROWS:(c + 1) * PERM_ROWS] = (
                lax.dot_general(wrh, h_hi, nt, preferred_element_type=_F32)
                + lax.dot_general(wrh, h_lo, nt, preferred_element_type=_F32)
                + lax.dot_general(wrl_ref[...], h_hi, nt, preferred_element_type=_F32))

        n_chunks = tm // PERM_ROWS
        hs = mix(0)
        for c in range(n_chunks):
            hs_next = mix(c + 1) if c + 1 < n_chunks else None
            logits(c, *hs)
            hs = hs_next
        gates = _route(lg_ref[0:ROUTER_ROWS, :] + jnp.tile(br_ref[...], (1, tm // LANES))[:ROUTER_ROWS])
        gt = jnp.concatenate([gates, jnp.zeros((LANES - ROUTER_ROWS, tm), _F32)], axis=0)

        gid = gt[GROUP_ROW:GROUP_ROW + 1, :].astype(jnp.int32)
        grow = lax.broadcasted_iota(jnp.int32, (8, tm), 0)
        onehot = jnp.where(grow == gid, 1.0, 0.0)
        rank = _dot(onehot.astype(_BF16), ut_ref[...])
        goff = jnp.int32(0)
        off_rows = jnp.zeros((8, tm), _F32)
        for g in range(N_GROUPS):
            cnt = jnp.sum(jnp.where(grow == g, onehot, 0.0)).astype(jnp.int32)
            meta_ref[g] = goff
            meta_ref[N_GROUPS + g] = cnt
            off_rows = jnp.where(grow == g, goff.astype(_F32), off_rows)
            goff = goff + (cnt + GROUP_ALIGN - 1) // GROUP_ALIGN * GROUP_ALIGN
        pos = jnp.sum(onehot * (rank + off_rows), axis=0, keepdims=True)
        pos_ref[...] = jnp.transpose(jnp.broadcast_to(pos, (LANES, tm)))
        pos_i = pos.astype(jnp.int32)
        g_hi, g_lo = _split_bf16(gt)
        g_hl = jnp.concatenate([g_hi, g_lo], axis=0)
        h = h_ref[...]
        for c in range(tms_p // PERM_ROWS):
            prow = lax.broadcasted_iota(jnp.int32, (PERM_ROWS, tm), 0) + c * PERM_ROWS
            perm = jnp.where(prow == pos_i, 1.0, 0.0).astype(_BF16)
            rows = pl.ds(c * PERM_ROWS, PERM_ROWS)
            hs_ref[rows, :] = _dot(perm, h).astype(_BF16)
            g2 = lax.dot_general(perm, g_hl, nt, preferred_element_type=_F32)
            gs_ref[rows, :] = g2[:, :LANES] + g2[:, LANES:]
        hs_ref[tms_p:, :] = jnp.zeros((tms - tms_p, D_MODEL), _BF16)
        gs_ref[tms_p:, :] = jnp.zeros((tms - tms_p, LANES), _F32)
        ys_ref[...] = jnp.zeros_like(ys_ref)

    grp = s // steps_per_group
    goff = meta_ref[grp]
    cnt = meta_ref[N_GROUPS + grp]

    def ffn(r0, m):
        rows = pl.ds(pl.multiple_of(r0, GROUP_ALIGN), m)
        hb = hs_ref[rows, :]
        gsb = gs_ref[rows, :]
        lane = lax.broadcasted_iota(jnp.int32, (m, LANES), 1)
        acc = jnp.zeros((m, D_MODEL), _F32)
        gus = [_dot(hb, win_ref[j]) for j in range(EXPERT_STEP)]
        acts = [(jax.nn.silu(gu[:, :D_EXPERT]) * gu[:, D_EXPERT:]).astype(_BF16) for gu in gus]
        for j in range(EXPERT_STEP):
            gate = jnp.sum(jnp.where(lane == s * EXPERT_STEP + j, gsb, 0.0), axis=-1, keepdims=True)
            acc = acc + gate * _dot(acts[j], wout_ref[j])
        ys_ref[rows, :] += acc

    for idx, m in enumerate(ffn_sizes):
        lo = ffn_sizes[idx - 1] if idx else 0

        @pl.when((cnt > lo) & (cnt <= m))
        def _(m=m):
            ffn(goff, m)

    @pl.when(cnt > ffn_sizes[-1])
    def _():
        m = ffn_sizes[len(ffn_sizes) // 2]

        def block(b, carry):
            ffn(goff + b * m, m)
            return carry

        lax.fori_loop(0, (cnt + m - 1) // m, block, 0)

    @pl.when(s == pl.num_programs(1) - 1)
    def _():
        hs_ref[0:tms_p, :] = ys_ref[0:tms_p, :].astype(_BF16)
        for c in range(tm // PERM_ROWS):
            rows = pl.ds(c * PERM_ROWS, PERM_ROWS)
            pos_c = jnp.tile(pos_ref[rows, :], (1, tms_p // LANES)).astype(jnp.int32)
            unperm = jnp.where(lax.broadcasted_iota(jnp.int32, (PERM_ROWS, tms_p), 1) == pos_c,
                               1.0, 0.0).astype(_BF16)
            y_ref[rows, :] = _rms(y_ref[rows, :] + _dot(unperm, hs_ref[0:tms_p, :]), gf_ref[...])


def _tail(a, b_out, x, wa, wb, g_ffn, wr_hi, wr_lo, br, w_e_in, w_e_out, g_final, *, tm, off):
    bsz, t_a, _ = a.shape
    rows, d = x.shape
    t_x = rows // bsz
    assert t_x % tm == 0 and tm % PERM_ROWS == 0
    tiles_per_batch = t_x // tm
    tms_p = pl.cdiv(tm + N_GROUPS * (GROUP_ALIGN - 1), PERM_ROWS) * PERM_ROWS
    unit = max(GROUP_ALIGN, tm // (8 * N_GROUPS))
    ffn_sizes = tuple(unit * f for f in range(7, 13))
    tms = tms_p + ffn_sizes[-1]
    n_steps = N_EXPERTS // EXPERT_STEP
    res = pl.BlockSpec((1, t_a, CONV_CH), lambda i, s: (i // tiles_per_batch, 0, 0))
    tok = pl.BlockSpec((tm, d), lambda i, s: (i, 0))
    const = lambda shape: pl.BlockSpec(shape, lambda i, s: (0,) * len(shape), pipeline_mode=pl.Buffered(1))
    return pl.pallas_call(
        functools.partial(_tail_kernel, tm=tm, tms_p=tms_p, ffn_sizes=ffn_sizes, off=off,
                          tiles_per_batch=tiles_per_batch),
        grid=(rows // tm, n_steps),
        in_specs=[res, res, tok, const(wa.shape), const(wb.shape), const((1, d)),
                  const(wr_hi.shape), const(wr_lo.shape), const((LANES, LANES)),
                  pl.BlockSpec((EXPERT_STEP, d, 2 * D_EXPERT), lambda i, s: (s, 0, 0)),
                  pl.BlockSpec((EXPERT_STEP, D_EXPERT, d), lambda i, s: (s, 0, 0)),
                  const((1, d))],
        out_specs=tok,
        out_shape=jax.ShapeDtypeStruct((rows, d), _F32),
        scratch_shapes=[pltpu.VMEM((tm, d), _BF16),
                        pltpu.VMEM((LANES, tm), _F32),
                        pltpu.VMEM((tms, d), _BF16),
                        pltpu.VMEM((tms, LANES), _F32),
                        pltpu.VMEM((tms, d), _F32),
                        pltpu.VMEM((tm, tm), _BF16),
                        pltpu.VMEM((tm, LANES), _F32),
                        pltpu.SMEM((2 * N_GROUPS,), jnp.int32)],
        compiler_params=pltpu.CompilerParams(
            dimension_semantics=("arbitrary", "arbitrary"), vmem_limit_bytes=MOE_VMEM_LIMIT),
        name=f"tail_r{rows}",
    )(a, b_out, x, wa, wb, g_ffn, wr_hi, wr_lo, br, w_e_in, w_e_out, g_final)


def kernel(x_prompt, x_sample, cache_k, cache_v, cache_conv, meta_tokens, g_mix, w_in, conv_w, conv_b,
           conv_ln_g, conv_ln_b, g_sb, w_out, g_ffn, w_router_group, b_router_group, w_router_expert,
           b_router_expert, w_expert_in, w_expert_out, g_final):
    depth = w_in.shape[0]
    assert depth == 1, "single-layer trunk"
    bp, tp, d = x_prompt.shape
    bs, ts, _ = x_sample.shape
    n_past = cache_k.shape[2]

    w_in_b = w_in[0].astype(_BF16)
    wa = w_out[0, :CONV_CH].astype(_BF16)
    wb = w_out[0, CONV_CH:].astype(_BF16)
    w_e_in = w_expert_in[0].astype(_BF16)
    w_e_out = w_expert_out[0].astype(_BF16)
    pad = LANES - N_EXPERTS - N_GROUPS
    wr = jnp.concatenate([w_router_expert[0].T, w_router_group[0].T, jnp.zeros((pad, d), _F32)], axis=0)
    wr_hi = wr.astype(_BF16)
    wr_lo = (wr - wr_hi.astype(_F32)).astype(_BF16)
    br = jnp.concatenate([b_router_expert[0], b_router_group[0], jnp.zeros((pad,), _F32)])
    br = jnp.broadcast_to(br[:, None], (LANES, LANES))
    g_mix2, g_ffn2, g_fin2, g_sb2 = g_mix[0][None], g_ffn[0][None], g_final[None], g_sb[0][None]
    cb, lg, lb = conv_b[0][None], conv_ln_g[0][None], conv_ln_b[0][None]

    def tail(a, b_out, x, *, off, tm):
        rows = x.shape[0] * x.shape[1]
        return _tail(a, b_out, x.reshape(rows, d), wa, wb, g_ffn2, wr_hi, wr_lo, br, w_e_in, w_e_out, g_fin2,
                     tm=tm, off=off).reshape(x.shape)

    a_p, q_p, k_p, v_p, nc_p = _front(x_prompt, meta_tokens, g_mix2, w_in_b,
                                      jnp.zeros((bp, HIST_PAD, CONV_CH), _F32), conv_w[0], cb, lg, lb,
                                      n_tiles=3, chunk_rows=256)
    b_p = _stick_breaking(q_p, k_p, v_p, None, None, g_sb2, mq=KEY_BLOCK, g_lock=8, q_off=N_META,
                          front_pad=2 * KEY_BLOCK - N_META)
    y_p = tail(a_p, b_p, x_prompt, off=N_META, tm=1024)

    hist = jnp.pad(cache_conv[0], ((0, 0), (HIST_PAD - CONV_HIST, 0), (0, 0)))
    a_s, q_s, k_s, v_s, nc_s = _front(x_sample, None, g_mix2, w_in_b, hist, conv_w[0], cb, lg, lb,
                                      n_tiles=1, chunk_rows=ts)
    b_s = _stick_breaking(q_s, k_s, v_s, cache_k[0].reshape(bs, n_past, SB_WIDTH),
                          cache_v[0].reshape(bs, n_past, SB_WIDTH), g_sb2, mq=ts, g_lock=1, q_off=0,
                          front_pad=KEY_BLOCK)
    flat = lambda t: t.reshape(1, bs * ts, t.shape[-1])
    y_s = tail(flat(a_s), flat(b_s), flat(x_sample), off=0, tm=bs * ts).reshape(x_sample.shape)

    heads = (SB_HEADS, SB_HEAD_DIM)
    lead = HIST_PAD - CONV_HIST
    return (y_p, y_s,
            k_p.reshape(1, bp, tp + N_META, *heads), v_p.reshape(1, bp, tp + N_META, *heads),
            nc_p[None, :, lead:],
            k_s.reshape(1, bs, ts, *heads), v_s.reshape(1, bs, ts, *heads),
            nc_s[None, :, lead:])
```

```python
import functools

import jax
import jax.numpy as jnp
from jax import lax
from jax.experimental import pallas as pl
from jax.experimental.pallas import tpu as pltpu

D_MODEL = 1024
N_META = 16
CONV_CH = 512
CONV_WIDTH = 31
CONV_HIST = CONV_WIDTH - 1
SB_HEADS = 8
SB_HEAD_DIM = 64
SB_WIDTH = SB_HEADS * SB_HEAD_DIM
N_GROUPS = 4
EXPERTS_PER_GROUP = 8
N_EXPERTS = N_GROUPS * EXPERTS_PER_GROUP
D_EXPERT = 256
EPS = 1e-6

LANES = 128
HIST_PAD = 32
KEY_BLOCK = 128
KEY_CHUNK = 256
LOG2_E = 1.4426950408889634
MASKED_Z = -1e30
EXIT_LOG2 = 128.0
ROUTER_ROWS = 40
GROUP_ROW = N_EXPERTS
GROUP_ALIGN = 16
PERM_ROWS = 256
EXPERT_STEP = 4
VMEM_LIMIT = 52 * 1024 * 1024
MOE_VMEM_LIMIT = 58 * 1024 * 1024

_F32 = jnp.float32
_BF16 = jnp.bfloat16


def _dot(a, b):
    return jnp.dot(a, b, preferred_element_type=_F32)


def _rms(x, g):
    return x * lax.rsqrt(jnp.mean(x * x, axis=-1, keepdims=True) + EPS) * g


def _conv_chunk(win, cwb_ref, rows):
    lead = HIST_PAD - CONV_HIST
    tiles = rows // 8
    acc = jnp.zeros((tiles, 8, CONV_CH), _F32)
    for s in range(8):
        taps = [k for k in range(CONV_WIDTH) if (lead + k) % 8 == s]
        shifted = win if s == 0 else win[s:s + rows + HIST_PAD - 8, :]
        for k in taps:
            off = lead + k - s
            tap = shifted[off:off + rows, :].reshape(tiles, 8, CONV_CH)
            acc = acc + cwb_ref[8 * k:8 * k + 8, :][None] * tap
    return acc.reshape(rows, CONV_CH)


def _front_kernel(*refs, tile, has_meta, chunks):
    if has_meta:
        x_ref, meta_ref = refs[:2]
        refs = refs[2:]
    else:
        x_ref = refs[0]
        refs = refs[1:]
    (g_ref, w_ref, hist_ref, cw_ref, cb_ref, lg_ref, lb_ref,
     a_ref, q_ref, k_ref, v_ref, nc_ref, h_ref, uh_ref, cwb_ref) = refs
    t = pl.program_id(1)
    g = g_ref[...]
    if has_meta:
        main0 = pl.multiple_of(t * tile, 8)
        h_ref[N_META:, :] = _rms(x_ref[0, pl.ds(main0, tile - N_META), :], g).astype(_BF16)
        head0 = pl.multiple_of(jnp.maximum(t * tile - N_META, 0), 8)
        xh = jnp.where(t == 0, meta_ref[...], x_ref[0, pl.ds(head0, N_META), :])
        h_ref[0:N_META, :] = _rms(xh, g).astype(_BF16)
    else:
        h_ref[...] = _rms(x_ref[0], g).astype(_BF16)

    @pl.when(t == 0)
    def _():
        uh_ref[0:HIST_PAD, :] = hist_ref[0]

    for k in range(CONV_WIDTH):
        cwb_ref[8 * k:8 * k + 8, :] = jnp.broadcast_to(cw_ref[k:k + 1, :], (8, CONV_CH))

    c, s = CONV_CH, SB_WIDTH

    def proj(r0, rows):
        h = h_ref[r0:r0 + rows, :]
        uh_ref[HIST_PAD + r0:HIST_PAD + r0 + rows, :] = (
            _dot(h, w_ref[:, 0:c]) * jax.nn.sigmoid(_dot(h, w_ref[:, c:2 * c])))
        q_ref[0, r0:r0 + rows, :] = (_dot(h, w_ref[:, 2 * c:2 * c + s])
                                     * (SB_HEAD_DIM ** -0.5 * LOG2_E)).astype(_BF16)
        k_ref[0, r0:r0 + rows, :] = _dot(h, w_ref[:, 2 * c + s:2 * c + 2 * s])
        v_ref[0, r0:r0 + rows, :] = _dot(h, w_ref[:, 2 * c + 2 * s:2 * c + 3 * s])

    cb, lg, lb = cb_ref[...], lg_ref[...], lb_ref[...]

    def conv(r0, rows):
        acc = _conv_chunk(uh_ref[r0:r0 + rows + HIST_PAD, :], cwb_ref, rows) + cb
        mu = jnp.mean(acc, axis=-1, keepdims=True)
        dlt = acc - mu
        var = jnp.mean(dlt * dlt, axis=-1, keepdims=True)
        y = dlt * lax.rsqrt(var + EPS) * lg + lb
        a_ref[0, r0:r0 + rows, :] = (y * jax.nn.sigmoid(y)).astype(_BF16)

    proj(*chunks[0])
    for i, chunk in enumerate(chunks):
        if i + 1 < len(chunks):
            proj(*chunks[i + 1])
        conv(*chunk)

    last = uh_ref[tile:tile + HIST_PAD, :]
    nc_ref[0] = last
    uh_ref[0:HIST_PAD, :] = last


def _front(x, meta, g_mix, w_in, hist_padded, conv_w, conv_b, ln_g, ln_b, *, n_tiles, chunk_rows):
    b, t_x, d = x.shape
    t_all = t_x + (N_META if meta is not None else 0)
    tile = t_all // n_tiles
    assert tile * n_tiles == t_all and tile % 16 == 0 and tile >= HIST_PAD
    chunks = tuple((r0, min(chunk_rows, tile - r0)) for r0 in range(0, tile, chunk_rows))
    assert all(rows % 16 == 0 for _, rows in chunks)
    in_w = w_in.shape[1]
    const = lambda shape: pl.BlockSpec(shape, lambda i, j: (0,) * len(shape))
    in_specs = [pl.BlockSpec((1, t_x, d), lambda i, j: (i, 0, 0))]
    args = [x]
    if meta is not None:
        in_specs.append(const((N_META, d)))
        args.append(meta)
    in_specs += [const((1, d)), const((d, in_w)), pl.BlockSpec((1, HIST_PAD, CONV_CH), lambda i, j: (i, 0, 0)),
                 const((CONV_WIDTH, CONV_CH)), const((1, CONV_CH)), const((1, CONV_CH)), const((1, CONV_CH))]
    args += [g_mix, w_in, hist_padded, conv_w, conv_b, ln_g, ln_b]
    row_spec = pl.BlockSpec((1, tile, CONV_CH), lambda i, j: (i, j, 0))
    return pl.pallas_call(
        functools.partial(_front_kernel, tile=tile, has_meta=meta is not None, chunks=chunks),
        grid=(b, n_tiles),
        in_specs=in_specs,
        out_specs=[row_spec] * 4 + [pl.BlockSpec((1, HIST_PAD, CONV_CH), lambda i, j: (i, 0, 0))],
        out_shape=[jax.ShapeDtypeStruct((b, t_all, CONV_CH), _BF16),
                   jax.ShapeDtypeStruct((b, t_all, SB_WIDTH), _BF16),
                   jax.ShapeDtypeStruct((b, t_all, SB_WIDTH), _F32),
                   jax.ShapeDtypeStruct((b, t_all, SB_WIDTH), _F32),
                   jax.ShapeDtypeStruct((b, HIST_PAD, CONV_CH), _F32)],
        scratch_shapes=[pltpu.VMEM((tile, d), _BF16),
                        pltpu.VMEM((HIST_PAD + tile, CONV_CH), _F32),
                        pltpu.VMEM((8 * CONV_WIDTH, CONV_CH), _F32)],
        compiler_params=pltpu.CompilerParams(
            dimension_semantics=("arbitrary", "arbitrary"), vmem_limit_bytes=VMEM_LIMIT),
        name=f"front_t{t_all}",
    )(*args)


def _sb_kernel(*refs, mq, g_lock, n_super, q_off, front_pad, n_past):
    if n_past:
        q_ref, kn_ref, vn_ref, kp_ref, vp_ref, g_ref, tri_ref, o_ref = refs[:8]
    else:
        q_ref, kn_ref, vn_ref, g_ref, tri_ref, o_ref = refs[:6]
    kb_ref, v0_ref, v1_ref, acc_ref, carry_ref = refs[-5:]
    t_new = kn_ref.shape[1]
    rows = kb_ref.shape[0]
    lane_row = lax.broadcasted_iota(jnp.int32, (1, LANES), 1) < SB_HEAD_DIM

    def fill(r0, n, k, v):
        kb_ref[r0:r0 + n, :] = k.astype(_BF16)
        v0_ref[r0:r0 + n, :] = jnp.where(lane_row, v, 0.0).astype(_BF16)
        v1_ref[r0:r0 + n, :] = jnp.where(lane_row, 0.0, v).astype(_BF16)

    new0 = front_pad + n_past
    if front_pad:
        fill(0, front_pad, jnp.zeros((front_pad, LANES), _F32), jnp.zeros((front_pad, LANES), _F32))
    if n_past:
        fill(front_pad, n_past, kp_ref[0], vp_ref[0])
    fill(new0, t_new, kn_ref[0], vn_ref[0])
    tail = rows - new0 - t_new
    if tail:
        fill(new0 + t_new, tail, jnp.zeros((tail, LANES), _F32), jnp.zeros((tail, LANES), _F32))
    if q_off:
        o_ref[0, 0:q_off, :] = jnp.zeros((q_off, LANES), _BF16)

    tri = tri_ref[...]
    gsb = g_ref[...]
    lane_lo = lax.broadcasted_iota(jnp.int32, (mq, LANES), 1) < SB_HEAD_DIM
    col_minus_row = (lax.broadcasted_iota(jnp.int32, (2 * mq, KEY_BLOCK), 1) + KEY_BLOCK
                     - lax.broadcasted_iota(jnp.int32, (2 * mq, KEY_BLOCK), 0) % mq)
    assert (new0 + q_off) % KEY_BLOCK == 0 and front_pad >= KEY_BLOCK
    assert mq == KEY_BLOCK or g_lock * n_super == 1
    end0 = new0 + q_off + KEY_BLOCK
    nt_dims = (((1,), (1,)), ((), ()))

    def superblock(sb, unused):
        i0 = sb * g_lock
        q0s = [pl.multiple_of(q_off + (i0 + g) * mq, 16) for g in range(g_lock)]
        qhs = []
        for g in range(g_lock):
            qb = q_ref[0, pl.ds(q0s[g], mq), :]
            zero_q = jnp.zeros_like(qb)
            qhs.append(jnp.concatenate([jnp.where(lane_lo, qb, zero_q), jnp.where(lane_lo, zero_q, qb)],
                                       axis=0))
        acc_ref[...] = jnp.zeros_like(acc_ref)
        carry_ref[...] = jnp.zeros_like(carry_ref)

        def sweep(state):
            n = state[0]
            low = jnp.full((1, LANES), jnp.inf, _F32)
            starts = [end0 + (i0 + g) * mq - (n + 1) * KEY_CHUNK for g in range(g_lock)]
            r0s = [pl.multiple_of(jnp.maximum(st, 0), KEY_BLOCK) for st in starts]
            zs, cts = [], []
            for g in range(g_lock):
                z = lax.dot_general(qhs[g], kb_ref[pl.ds(r0s[g], KEY_CHUNK), :], nt_dims,
                                    preferred_element_type=_F32)
                hidden_r = jnp.where(col_minus_row < KEY_BLOCK + n * KEY_CHUNK, z[:, KEY_BLOCK:], MASKED_Z)
                zs.append(jnp.concatenate([z[:, :KEY_BLOCK], hidden_r], axis=1))
            for g in range(g_lock):
                z = zs[g]
                s = jnp.maximum(z, 0.0) + jnp.log2(1.0 + jnp.exp2(-jnp.abs(z)))
                hi = s.astype(_BF16)
                lo = (s - hi.astype(_F32)).astype(_BF16)
                cts.append(_dot(jnp.concatenate(
                    [jnp.concatenate([hi[:, KEY_BLOCK:], lo[:, KEY_BLOCK:]], axis=1),
                     jnp.concatenate([hi[:, :KEY_BLOCK], lo[:, :KEY_BLOCK]], axis=1)], axis=0), tri))
            for g in range(g_lock):
                ct_r, ct_l = cts[g][:2 * mq], cts[g][2 * mq:]
                carry = carry_ref[g]
                c_r = ct_r[:, :KEY_BLOCK] + carry
                carry = carry + ct_r[:, KEY_BLOCK:]
                c_l = ct_l[:, :KEY_BLOCK] + carry
                carry = carry + ct_l[:, KEY_BLOCK:]
                carry_ref[g] = carry
                w = jnp.exp2(zs[g] - jnp.concatenate([c_l, c_r], axis=1)).astype(_BF16)
                pv = (_dot(w[:mq], v0_ref[pl.ds(r0s[g], KEY_CHUNK), :])
                      + _dot(w[mq:], v1_ref[pl.ds(r0s[g], KEY_CHUNK), :]))
                acc_ref[g] += jnp.where(starts[g] >= 0, pv, 0.0)
                done = jnp.where(starts[g] - KEY_CHUNK >= 0, 0.0, 2 * EXIT_LOG2)
                low = jnp.minimum(low, jnp.min(carry, axis=0, keepdims=True) + done)
            return n + 1, jnp.where(jnp.min(low) > EXIT_LOG2, 0, 1)

        lax.while_loop(lambda st: st[1] > 0, sweep, (jnp.int32(0), jnp.int32(1)))

        for g in range(g_lock):
            o = acc_ref[g]
            o2 = o * o
            s_all = jnp.sum(o2, axis=-1, keepdims=True)
            s_lo = jnp.sum(jnp.where(lane_lo, o2, 0.0), axis=-1, keepdims=True)
            inv = jnp.where(lane_lo, lax.rsqrt(s_lo / SB_HEAD_DIM + EPS),
                            lax.rsqrt((s_all - s_lo) / SB_HEAD_DIM + EPS))
            o_ref[0, pl.ds(q0s[g], mq), :] = (o * inv * gsb).astype(_BF16)
        return unused

    lax.fori_loop(0, n_super, superblock, 0)


def _tri_matrix():
    r = lax.broadcasted_iota(jnp.int32, (2 * KEY_BLOCK, 2 * KEY_BLOCK), 0) % KEY_BLOCK
    c = lax.broadcasted_iota(jnp.int32, (2 * KEY_BLOCK, 2 * KEY_BLOCK), 1)
    return jnp.where((c >= KEY_BLOCK) | (r >= c), 1.0, 0.0).astype(_BF16)


def _stick_breaking(q, k_new, v_new, k_past, v_past, g_sb, *, mq, g_lock, q_off, front_pad):
    b, t_new, _ = k_new.shape
    n_past = 0 if k_past is None else k_past.shape[1]
    n_super = (t_new - q_off) // (mq * g_lock)
    assert n_super * mq * g_lock == t_new - q_off
    last_end = front_pad + n_past + q_off + KEY_BLOCK + (n_super * g_lock - 1) * mq
    rows = max(pl.cdiv(front_pad + n_past + t_new, KEY_BLOCK) * KEY_BLOCK, last_end)
    pairs = SB_WIDTH // LANES
    new_spec = pl.BlockSpec((1, t_new, LANES), lambda i, p: (i, 0, p))
    in_specs = [new_spec, new_spec, new_spec]
    args = [q, k_new, v_new]
    if n_past:
        past_spec = pl.BlockSpec((1, n_past, LANES), lambda i, p: (i, 0, p))
        in_specs += [past_spec, past_spec]
        args += [k_past, v_past]
    in_specs += [pl.BlockSpec((1, LANES), lambda i, p: (0, p)),
                 pl.BlockSpec((2 * KEY_BLOCK, 2 * KEY_BLOCK), lambda i, p: (0, 0))]
    args += [g_sb, _tri_matrix()]
    return pl.pallas_call(
        functools.partial(_sb_kernel, mq=mq, g_lock=g_lock, n_super=n_super, q_off=q_off,
                          front_pad=front_pad, n_past=n_past),
        grid=(b, pairs),
        in_specs=in_specs,
        out_specs=pl.BlockSpec((1, t_new, LANES), lambda i, p: (i, 0, p)),
        out_shape=jax.ShapeDtypeStruct((b, t_new, SB_WIDTH), _BF16),
        scratch_shapes=[pltpu.VMEM((rows, LANES), _BF16)] * 3
        + [pltpu.VMEM((g_lock, mq, LANES), _F32), pltpu.VMEM((g_lock, 2 * mq, LANES), _F32)],
        compiler_params=pltpu.CompilerParams(
            dimension_semantics=("arbitrary", "arbitrary"), vmem_limit_bytes=VMEM_LIMIT),
        name=f"stick_breaking_t{t_new}",
    )(*args)


def _split_bf16(x):
    hi = x.astype(_BF16)
    return hi, (x - hi.astype(_F32)).astype(_BF16)


def _route(logits):
    n = logits.shape[1]
    row = lax.broadcasted_iota(jnp.int32, (ROUTER_ROWS, n), 0)
    neg = -jnp.inf
    big = 4 * LANES
    is_g = (row >= N_EXPERTS) & (row < N_EXPERTS + N_GROUPS)
    lg = jnp.where(is_g, logits, neg)
    g_max = jnp.max(lg, axis=0, keepdims=True)
    p_g_sel = 1.0 / jnp.sum(jnp.where(is_g, jnp.exp(lg - g_max), 0.0), axis=0, keepdims=True)
    g_sel = jnp.min(jnp.where(lg == g_max, row, big), axis=0, keepdims=True) - N_EXPERTS
    in_grp = (row < N_EXPERTS) & ((row // EXPERTS_PER_GROUP) == g_sel)
    le = jnp.where(in_grp, logits, neg)
    e_max = jnp.max(le, axis=0, keepdims=True)
    pe = jnp.where(in_grp, jnp.exp(le - e_max), 0.0)
    pe = pe / jnp.sum(pe, axis=0, keepdims=True)
    pe = jnp.where(in_grp, pe, -1.0)
    top1 = jnp.max(pe, axis=0, keepdims=True)
    i1 = jnp.min(jnp.where(pe == top1, row, big), axis=0, keepdims=True)
    pe_rest = jnp.where(row == i1, -1.0, pe)
    top2 = jnp.max(pe_rest, axis=0, keepdims=True)
    i2 = jnp.min(jnp.where(pe_rest == top2, row, big), axis=0, keepdims=True)
    denom = top1 + top2
    gates = jnp.where(row == i1, top1 / denom, jnp.where(row == i2, top2 / denom, 0.0)) * p_g_sel
    return jnp.where(row == GROUP_ROW, g_sel.astype(_F32), gates)


def _tail_kernel(a_ref, b_ref, x_ref, wa_ref, wb_ref, gffn_ref, wrh_ref, wrl_ref, br_ref,
                 win_ref, wout_ref, gf_ref, y_ref,
                 h_ref, lg_ref, hs_ref, gs_ref, ys_ref, ut_ref, pos_ref, meta_ref,
                 *, tm, tms_p, ffn_sizes, off, tiles_per_batch):
    i = pl.program_id(0)
    s = pl.program_id(1)
    steps_per_group = EXPERTS_PER_GROUP // EXPERT_STEP
    nt = (((1,), (1,)), ((), ()))
    tms = hs_ref.shape[0]

    @pl.when((i == 0) & (s == 0))
    def _():
        ut_ref[...] = jnp.where(lax.broadcasted_iota(jnp.int32, (tm, tm), 0)
                                < lax.broadcasted_iota(jnp.int32, (tm, tm), 1), 1.0, 0.0).astype(_BF16)

    @pl.when(s == 0)
    def _():
        a0 = off + (i % tiles_per_batch) * tm
        wrh = wrh_ref[...]
        def mix(c):
            rows = pl.ds(c * PERM_ROWS, PERM_ROWS)
            arows = pl.ds(pl.multiple_of(a0 + c * PERM_ROWS, 16), PERM_ROWS)
            x2 = (x_ref[rows, :] + _dot(a_ref[0, arows, :], wa_ref[...])
                  + _dot(b_ref[0, arows, :], wb_ref[...]))
            y_ref[rows, :] = x2
            h_hi, h_lo = _split_bf16(_rms(x2, gffn_ref[...]))
            h_ref[rows, :] = h_hi
            return h_hi, h_lo

        def logits(c, h_hi, h_lo):
            lg_ref[:, c * PERM_ROWS:(c + 1) * PERM_ROWS] = (
                lax.dot_general(wrh, h_hi, nt, preferred_element_type=_F32)
                + lax.dot_general(wrh, h_lo, nt, preferred_element_type=_F32)
                + lax.dot_general(wrl_ref[...], h_hi, nt, preferred_element_type=_F32))

        n_chunks = tm // PERM_ROWS
        hs = mix(0)
        for c in range(n_chunks):
            hs_next = mix(c + 1) if c + 1 < n_chunks else None
            logits(c, *hs)
            hs = hs_next
        gates = _route(lg_ref[0:ROUTER_ROWS, :] + jnp.tile(br_ref[...], (1, tm // LANES))[:ROUTER_ROWS])
        gt = jnp.concatenate([gates, jnp.zeros((LANES - ROUTER_ROWS, tm), _F32)], axis=0)

        gid = gt[GROUP_ROW:GROUP_ROW + 1, :].astype(jnp.int32)
        grow = lax.broadcasted_iota(jnp.int32, (8, tm), 0)
        onehot = jnp.where(grow == gid, 1.0, 0.0)
        rank = _dot(onehot.astype(_BF16), ut_ref[...])
        goff = jnp.int32(0)
        off_rows = jnp.zeros((8, tm), _F32)
        for g in range(N_GROUPS):
            cnt = jnp.sum(jnp.where(grow == g, onehot, 0.0)).astype(jnp.int32)
            meta_ref[g] = goff
            meta_ref[N_GROUPS + g] = cnt
            off_rows = jnp.where(grow == g, goff.astype(_F32), off_rows)
            goff = goff + (cnt + GROUP_ALIGN - 1) // GROUP_ALIGN * GROUP_ALIGN
        pos = jnp.sum(onehot * (rank + off_rows), axis=0, keepdims=True)
        pos_ref[...] = jnp.transpose(jnp.broadcast_to(pos, (LANES, tm)))
        pos_i = pos.astype(jnp.int32)
        g_hi, g_lo = _split_bf16(gt)
        g_hl = jnp.concatenate([g_hi, g_lo], axis=0)
        h = h_ref[...]
        for c in range(tms_p // PERM_ROWS):
            prow = lax.broadcasted_iota(jnp.int32, (PERM_ROWS, tm), 0) + c * PERM_ROWS
            perm = jnp.where(prow == pos_i, 1.0, 0.0).astype(_BF16)
            rows = pl.ds(c * PERM_ROWS, PERM_ROWS)
            hs_ref[rows, :] = _dot(perm, h).astype(_BF16)
            g2 = lax.dot_general(perm, g_hl, nt, preferred_element_type=_F32)
            gs_ref[rows, :] = g2[:, :LANES] + g2[:, LANES:]
        hs_ref[tms_p:, :] = jnp.zeros((tms - tms_p, D_MODEL), _BF16)
        gs_ref[tms_p:, :] = jnp.zeros((tms - tms_p, LANES), _F32)
        ys_ref[...] = jnp.zeros_like(ys_ref)

    grp = s // steps_per_group
    goff = meta_ref[grp]
    cnt = meta_ref[N_GROUPS + grp]

    def ffn(r0, m):
        rows = pl.ds(pl.multiple_of(r0, GROUP_ALIGN), m)
        hb = hs_ref[rows, :]
        gsb = gs_ref[rows, :]
        lane = lax.broadcasted_iota(jnp.int32, (m, LANES), 1)
        acc = jnp.zeros((m, D_MODEL), _F32)
        gus = [_dot(hb, win_ref[j]) for j in range(EXPERT_STEP)]
        acts = [(jax.nn.silu(gu[:, :D_EXPERT]) * gu[:, D_EXPERT:]).astype(_BF16) for gu in gus]
        for j in range(EXPERT_STEP):
            gate = jnp.sum(jnp.where(lane == s * EXPERT_STEP + j, gsb, 0.0), axis=-1, keepdims=True)
            acc = acc + gate * _dot(acts[j], wout_ref[j])
        ys_ref[rows, :] += acc

    for idx, m in enumerate(ffn_sizes):
        lo = ffn_sizes[idx - 1] if idx else 0

        @pl.when((cnt > lo) & (cnt <= m))
        def _(m=m):
            ffn(goff, m)

    @pl.when(cnt > ffn_sizes[-1])
    def _():
        m = ffn_sizes[len(ffn_sizes) // 2]

        def block(b, carry):
            ffn(goff + b * m, m)
            return carry

        lax.fori_loop(0, (cnt + m - 1) // m, block, 0)

    @pl.when(s == pl.num_programs(1) - 1)
    def _():
        hs_ref[0:tms_p, :] = ys_ref[0:tms_p, :].astype(_BF16)
        for c in range(tm // PERM_ROWS):
            rows = pl.ds(c * PERM_ROWS, PERM_ROWS)
            pos_c = jnp.tile(pos_ref[rows, :], (1, tms_p // LANES)).astype(jnp.int32)
            unperm = jnp.where(lax.broadcasted_iota(jnp.int32, (PERM_ROWS, tms_p), 1) == pos_c,
                               1.0, 0.0).astype(_BF16)
            y_ref[rows, :] = _rms(y_ref[rows, :] + _dot(unperm, hs_ref[0:tms_p, :]), gf_ref[...])


def _tail(a, b_out, x, wa, wb, g_ffn, wr_hi, wr_lo, br, w_e_in, w_e_out, g_final, *, tm, off):
    bsz, t_a, _ = a.shape
    rows, d = x.shape
    t_x = rows // bsz
    assert t_x % tm == 0 and tm % PERM_ROWS == 0
    tiles_per_batch = t_x // tm
    tms_p = pl.cdiv(tm + N_GROUPS * (GROUP_ALIGN - 1), PERM_ROWS) * PERM_ROWS
    unit = max(GROUP_ALIGN, tm // (8 * N_GROUPS))
    ffn_sizes = tuple(unit * f for f in range(7, 13))
    tms = tms_p + ffn_sizes[-1]
    n_steps = N_EXPERTS // EXPERT_STEP
    res = pl.BlockSpec((1, t_a, CONV_CH), lambda i, s: (i // tiles_per_batch, 0, 0))
    tok = pl.BlockSpec((tm, d), lambda i, s: (i, 0))
    const = lambda shape: pl.BlockSpec(shape, lambda i, s: (0,) * len(shape), pipeline_mode=pl.Buffered(1))
    return pl.pallas_call(
        functools.partial(_tail_kernel, tm=tm, tms_p=tms_p, ffn_sizes=ffn_sizes, off=off,
                          tiles_per_batch=tiles_per_batch),
        grid=(rows // tm, n_steps),
        in_specs=[res, res, tok, const(wa.shape), const(wb.shape), const((1, d)),
                  const(wr_hi.shape), const(wr_lo.shape), const((LANES, LANES)),
                  pl.BlockSpec((EXPERT_STEP, d, 2 * D_EXPERT), lambda i, s: (s, 0, 0)),
                  pl.BlockSpec((EXPERT_STEP, D_EXPERT, d), lambda i, s: (s, 0, 0)),
                  const((1, d))],
        out_specs=tok,
        out_shape=jax.ShapeDtypeStruct((rows, d), _F32),
        scratch_shapes=[pltpu.VMEM((tm, d), _BF16),
                        pltpu.VMEM((LANES, tm), _F32),
                        pltpu.VMEM((tms, d), _BF16),
                        pltpu.VMEM((tms, LANES), _F32),
                        pltpu.VMEM((tms, d), _F32),
                        pltpu.VMEM((tm, tm), _BF16),
                        pltpu.VMEM((tm, LANES), _F32),
                        pltpu.SMEM((2 * N_GROUPS,), jnp.int32)],
        compiler_params=pltpu.CompilerParams(
            dimension_semantics=("arbitrary", "arbitrary"), vmem_limit_bytes=MOE_VMEM_LIMIT),
        name=f"tail_r{rows}",
    )(a, b_out, x, wa, wb, g_ffn, wr_hi, wr_lo, br, w_e_in, w_e_out, g_final)


def kernel(x_prompt, x_sample, cache_k, cache_v, cache_conv, meta_tokens, g_mix, w_in, conv_w, conv_b,
           conv_ln_g, conv_ln_b, g_sb, w_out, g_ffn, w_router_group, b_router_group, w_router_expert,
           b_router_expert, w_expert_in, w_expert_out, g_final):
    depth = w_in.shape[0]
    assert depth == 1, "single-layer trunk"
    bp, tp, d = x_prompt.shape
    bs, ts, _ = x_sample.shape
    n_past = cache_k.shape[2]

    w_in_b = w_in[0].astype(_BF16)
    wa = w_out[0, :CONV_CH].astype(_BF16)
    wb = w_out[0, CONV_CH:].astype(_BF16)
    w_e_in = w_expert_in[0].astype(_BF16)
    w_e_out = w_expert_out[0].astype(_BF16)
    pad = LANES - N_EXPERTS - N_GROUPS
    wr = jnp.concatenate([w_router_expert[0].T, w_router_group[0].T, jnp.zeros((pad, d), _F32)], axis=0)
    wr_hi = wr.astype(_BF16)
    wr_lo = (wr - wr_hi.astype(_F32)).astype(_BF16)
    br = jnp.concatenate([b_router_expert[0], b_router_group[0], jnp.zeros((pad,), _F32)])
    br = jnp.broadcast_to(br[:, None], (LANES, LANES))
    g_mix2, g_ffn2, g_fin2, g_sb2 = g_mix[0][None], g_ffn[0][None], g_final[None], g_sb[0][None]
    cb, lg, lb = conv_b[0][None], conv_ln_g[0][None], conv_ln_b[0][None]

    def tail(a, b_out, x, *, off, tm):
        rows = x.shape[0] * x.shape[1]
        return _tail(a, b_out, x.reshape(rows, d), wa, wb, g_ffn2, wr_hi, wr_lo, br, w_e_in, w_e_out, g_fin2,
                     tm=tm, off=off).reshape(x.shape)

    a_p, q_p, k_p, v_p, nc_p = _front(x_prompt, meta_tokens, g_mix2, w_in_b,
                                      jnp.zeros((bp, HIST_PAD, CONV_CH), _F32), conv_w[0], cb, lg, lb,
                                      n_tiles=3, chunk_rows=256)
    b_p = _stick_breaking(q_p, k_p, v_p, None, None, g_sb2, mq=KEY_BLOCK, g_lock=8, q_off=N_META,
                          front_pad=2 * KEY_BLOCK - N_META)
    y_p = tail(a_p, b_p, x_prompt, off=N_META, tm=1024)

    hist = jnp.pad(cache_conv[0], ((0, 0), (HIST_PAD - CONV_HIST, 0), (0, 0)))
    a_s, q_s, k_s, v_s, nc_s = _front(x_sample, None, g_mix2, w_in_b, hist, conv_w[0], cb, lg, lb,
                                      n_tiles=1, chunk_rows=ts)
    b_s = _stick_breaking(q_s, k_s, v_s, cache_k[0].reshape(bs, n_past, SB_WIDTH),
                          cache_v[0].reshape(bs, n_past, SB_WIDTH), g_sb2, mq=ts, g_lock=1, q_off=0,
                          front_pad=KEY_BLOCK)
    flat = lambda t: t.reshape(1, bs * ts, t.shape[-1])
    y_s = tail(flat(a_s), flat(b_s), flat(x_sample), off=0, tm=bs * ts).reshape(x_sample.shape)

    heads = (SB_HEADS, SB_HEAD_DIM)
    lead = HIST_PAD - CONV_HIST
    return (y_p, y_s,
            k_p.reshape(1, bp, tp + N_META, *heads), v_p.reshape(1, bp, tp + N_META, *heads),
            nc_p[None, :, lead:],
            k_s.reshape(1, bs, ts, *heads), v_s.reshape(1, bs, ts, *heads),
            nc_s[None, :, lead:])
```
